```python
import math
import jax, jax.numpy as jnp
from jax import lax
import numpy as np

D_MODEL = 2048
BATCH = 8
SEQ = 2048
DEPTH = 1

GRID_W = 64
CTX_LEN = 256
D_MIX = D_MODEL
S5_W = D_MIX // 2
HY_W = D_MIX - S5_W
D_IN = S5_W + 3 * HY_W
S5_H = 16
S5_GROUPS = S5_W // S5_H
S5_P = 64
S5_DT_MIN = 1e-3
S5_DT_MAX = 1e-1
S5_RE_MAX = -1e-4
HY_BANDS = 16
HY_EMB = 2 * HY_BANDS + 1
HY_ORDER = 64
HY_FAST_DECAY = 0.3
HY_SLOW_DECAY = 1.5
HY_TARGET = 1e-2
N_EXPERTS = 32
TOP_K = 4
D_FF = D_MODEL
SWIGLU_LIMIT = 7.0
SWIGLU_ALPHA = 1.702
MOE_BLOCK = 256
N_MOD = 6
EPS = 1e-6

kernel_name = "hybrid_s5_hyena_moe_dit_block"


def rmsnorm(x, g):
    xf = x.astype(jnp.float32)
    y = xf * lax.rsqrt(jnp.mean(xf * xf, axis=-1, keepdims=True) + EPS)
    return (y * g.astype(jnp.float32)).astype(x.dtype)


def modulate(h, shift, scale):
    return h * (1 + scale) + shift


def short_conv_seq(u, w, b):
    up = jnp.pad(u, ((0, 0), (1, 1), (0, 0)))
    return up[:, :-2] * w[0] + up[:, 1:-1] * w[1] + up[:, 2:] * w[2] + b


def short_conv_grid(u, w, b):
    n, length, ch = u.shape
    rows = length // GRID_W
    g = u.reshape(n * rows, GRID_W, ch)
    return short_conv_seq(g, w, b).reshape(n, length, ch)


def hyena_filter(length, w1, b1, w2, b2, w3, b3, w4, freq, deltas):
    f32 = jnp.float32
    t = jnp.linspace(0.0, 1.0, length, dtype=f32)[:, None]
    bands = jnp.linspace(1e-4, HY_BANDS - 1, HY_BANDS, dtype=f32)
    ang = (2 * math.pi / length) * jnp.arange(length, dtype=f32)[:, None] * bands
    z = jnp.concatenate([t, jnp.cos(ang), -jnp.sin(ang)], axis=-1)
    fr = freq.astype(f32)
    h = jnp.sin(fr * (z @ w1.astype(f32) + b1.astype(f32)))
    h = jnp.sin(fr * (h @ w2.astype(f32) + b2.astype(f32)))
    h = jnp.sin(fr * (h @ w3.astype(f32) + b3.astype(f32)))
    h = (h @ w4.astype(f32)) * jnp.exp(-t * jnp.abs(deltas.astype(f32)))
    h_fwd = h[:, :HY_W]
    h_bwd = h[:length - 1, HY_W:]
    norm = jnp.sum(jnp.abs(h_fwd), axis=0) + jnp.sum(jnp.abs(h_bwd), axis=0)
    k = jnp.concatenate([h_fwd, jnp.zeros((1, HY_W), f32), h_bwd[::-1]], axis=0)
    return k / norm


def long_conv(u, k):
    length = u.shape[1]
    n = 2 * length
    uf = jnp.fft.rfft(u.astype(jnp.float32), n=n, axis=1)
    kf = jnp.fft.rfft(k, n=n, axis=0)
    return jnp.fft.irfft(uf * kf, n=n, axis=1)[:, :length]


def hyena_mixer(p, conv_fn, conv_w, conv_b, w1, b1, w2, b2, w3, b3, w4, freq, deltas, d_skip):
    z = conv_fn(p, conv_w, conv_b)
    v, x1, x0 = jnp.split(z, 3, axis=-1)
    k = hyena_filter(p.shape[1], w1, b1, w2, b2, w3, b3, w4, freq, deltas)
    v = v * x1
    v = long_conv(v, k).astype(p.dtype) + v * d_skip
    return v * x0


def _diag_combine(left, right):
    a_l, b_l = left
    a_r, b_r = right
    return a_l * a_r, a_r * b_l + b_r


def diag_scan(lam_bar, bu, reverse):
    a = jnp.broadcast_to(lam_bar, (bu.shape[0], 1) + lam_bar.shape)
    _, states = lax.associative_scan(_diag_combine, (a, bu), reverse=reverse, axis=0)
    return states


def s5_readout(y, w_glu, b_glu, dtype):
    nb, length = y.shape[:2]
    y = jax.nn.gelu(y.reshape(nb, length, S5_W), approximate=False)
    y = y * jax.nn.sigmoid(y @ w_glu.astype(jnp.float32) + b_glu.astype(jnp.float32))
    return y.astype(dtype)


def s5_mixer(u_lat, u_ctx, lam_re, lam_im, log_step, b_re, b_im, c_re, c_im, d_skip,
             w_glu, b_glu, with_ctx_out):
    f32 = jnp.float32
    nb, l_lat, _ = u_lat.shape
    l_ctx = u_ctx.shape[1]
    ul = u_lat.astype(f32).reshape(nb, l_lat, S5_GROUPS, S5_H)
    uc = u_ctx.astype(f32).reshape(nb, l_ctx, S5_GROUPS, S5_H)
    dsk = d_skip.astype(f32).reshape(S5_GROUPS, S5_H)
    y_lat = ul * dsk
    y_ctx = uc * dsk if with_ctx_out else None
    for direction in range(2):
        reverse = direction == 1
        lam = lax.complex(jnp.minimum(lam_re[direction].astype(f32), S5_RE_MAX),
                          lam_im[direction].astype(f32))
        lam_bar = jnp.exp(lam * jnp.exp(log_step[direction].astype(f32)))
        b_bar = ((lam_bar - 1) / lam)[..., None] * lax.complex(
            b_re[direction].astype(f32), b_im[direction].astype(f32))
        c_mat = lax.complex(c_re[direction].astype(f32), c_im[direction].astype(f32))
        st_ctx = diag_scan(lam_bar, jnp.einsum('blgh,gph->lbgp', uc, b_bar), reverse)
        s0 = st_ctx[0] if reverse else st_ctx[-1]
        bu_lat = jnp.einsum('blgh,gph->lbgp', ul, b_bar)
        first = l_lat - 1 if reverse else 0
        bu_lat = bu_lat.at[first].add(lam_bar * s0)
        st_lat = diag_scan(lam_bar, bu_lat, reverse)
        y_lat = y_lat + jnp.einsum('lbgp,ghp->blgh', st_lat, c_mat).real
        if with_ctx_out:
            y_ctx = y_ctx + jnp.einsum('lbgp,ghp->blgh', st_ctx, c_mat).real
    out_lat = s5_readout(y_lat, w_glu, b_glu, u_lat.dtype)
    out_ctx = s5_readout(y_ctx, w_glu, b_glu, u_ctx.dtype) if with_ctx_out else None
    return out_lat, out_ctx


def moe_ffn(h, w_router, b_router, w_gate_up, b_gate_up, w_down, b_down):
    lead = h.shape[:-1]
    d = h.shape[-1]
    t = h.reshape(-1, d)
    n_tok = t.shape[0]
    logits = (t @ w_router + b_router).astype(jnp.float32)
    top_val, top_idx = lax.top_k(logits, TOP_K)
    gate = jax.nn.softmax(top_val, axis=-1)
    flat_e = top_idx.reshape(-1)
    n_assign = n_tok * TOP_K
    order = jnp.argsort(flat_e)
    sorted_e = flat_e[order]
    tok = order // TOP_K
    counts = jnp.bincount(flat_e, length=N_EXPERTS)
    padded = (counts + MOE_BLOCK - 1) // MOE_BLOCK * MOE_BLOCK
    pad_end = jnp.cumsum(padded)
    pad_start = pad_end - padded
    start = jnp.cumsum(counts) - counts
    dest = pad_start[sorted_e] + jnp.arange(n_assign) - start[sorted_e]
    n_blocks = -(-n_assign // MOE_BLOCK) + N_EXPERTS
    src = jnp.full((n_blocks * MOE_BLOCK,), n_tok, jnp.int32).at[dest].set(tok)
    t_pad = jnp.concatenate([t, jnp.zeros((1, d), t.dtype)], axis=0)
    xb = t_pad[src].reshape(n_blocks, MOE_BLOCK, d)
    blk_e = jnp.minimum(jnp.searchsorted(pad_end, jnp.arange(n_blocks) * MOE_BLOCK, side='right'),
                        N_EXPERTS - 1)

    def expert_block(args):
        xblk, e = args
        gu = xblk @ w_gate_up[e] + b_gate_up[e]
        g, u = jnp.split(gu, 2, axis=-1)
        g = jnp.minimum(g, SWIGLU_LIMIT)
        u = jnp.clip(u, -SWIGLU_LIMIT, SWIGLU_LIMIT)
        return ((u + 1) * (g * jax.nn.sigmoid(SWIGLU_ALPHA * g))) @ w_down[e] + b_down[e]

    yb = lax.map(expert_block, (xb, blk_e)).reshape(-1, d)
    y = yb[dest] * gate.reshape(-1)[order][:, None].astype(yb.dtype)
    return jax.ops.segment_sum(y, tok, num_segments=n_tok).reshape(lead + (d,))


def setup_inputs(seed: int = 0) -> dict:
    key = jax.random.key(seed)
    keys = iter(jax.random.split(key, 48))
    f32 = jnp.float32

    def normal(shape, scale):
        return scale * jax.random.normal(next(keys), shape, f32)

    def gain(shape):
        return 1.0 + normal(shape, 0.05)

    s5_shape = (DEPTH, 2, S5_GROUPS, S5_P)
    n_idx = jnp.arange(S5_P, dtype=f32)
    decay_base = jnp.linspace(math.log(HY_TARGET) / HY_FAST_DECAY,
                              math.log(HY_TARGET) / HY_SLOW_DECAY, HY_W, dtype=f32)
    return {
        'x': normal((BATCH, SEQ, D_MODEL), 1.0),
        'c': normal((BATCH, D_MODEL), 1.0),
        'ctx': normal((BATCH, CTX_LEN, D_MODEL), 1.0),
        'c_ctx': normal((D_MODEL,), 1.0),
        'w_ada': normal((DEPTH, D_MODEL, N_MOD * D_MODEL), 0.5 * D_MODEL ** -0.5),
        'b_ada': normal((DEPTH, N_MOD * D_MODEL), 0.02),
        'g_norm1': gain((DEPTH, D_MODEL)),
        'g_norm2': gain((DEPTH, D_MODEL)),
        'w_in': normal((DEPTH, D_MODEL, D_IN), D_MODEL ** -0.5),
        's5_lam_re': -0.5 + normal(s5_shape, 0.02),
        's5_lam_im': math.pi * n_idx + normal(s5_shape, 0.02),
        's5_log_step': jax.random.uniform(next(keys), s5_shape, f32,
                                          math.log(S5_DT_MIN), math.log(S5_DT_MAX)),
        's5_b_re': normal((DEPTH, 2, S5_GROUPS, S5_P, S5_H), (2 * S5_H) ** -0.5),
        's5_b_im': normal((DEPTH, 2, S5_GROUPS, S5_P, S5_H), (2 * S5_H) ** -0.5),
        's5_c_re': normal((DEPTH, 2, S5_GROUPS, S5_H, S5_P), S5_P ** -0.5),
        's5_c_im': normal((DEPTH, 2, S5_GROUPS, S5_H, S5_P), S5_P ** -0.5),
        's5_d': normal((DEPTH, S5_W), 1.0),
        's5_w_glu': normal((DEPTH, S5_W, S5_W), S5_W ** -0.5),
        's5_b_glu': normal((DEPTH, S5_W), 0.02),
        'hy_conv_w': normal((DEPTH, 3, 3 * HY_W), 3 ** -0.5),
        'hy_conv_b': normal((DEPTH, 3 * HY_W), 0.02),
        'hy_w1': normal((DEPTH, HY_EMB, HY_ORDER), HY_EMB ** -0.5),
        'hy_b1': normal((DEPTH, HY_ORDER), 0.1),
        'hy_w2': normal((DEPTH, HY_ORDER, HY_ORDER), HY_ORDER ** -0.5),
        'hy_b2': normal((DEPTH, HY_ORDER), 0.1),
        'hy_w3': normal((DEPTH, HY_ORDER, HY_ORDER), HY_ORDER ** -0.5),
        'hy_b3': normal((DEPTH, HY_ORDER), 0.1),
        'hy_w4': normal((DEPTH, HY_ORDER, 2 * HY_W), HY_ORDER ** -0.5),
        'hy_freq': gain((DEPTH, HY_ORDER)),
        'hy_deltas': jnp.concatenate([decay_base, decay_base]) + normal((DEPTH, 2 * HY_W), 0.01),
        'hy_d': normal((DEPTH, HY_W), 1.0),
        'w_out': normal((DEPTH, D_MIX, D_MODEL), D_MIX ** -0.5),
        'w_router': normal((DEPTH, D_MODEL, N_EXPERTS), D_MODEL ** -0.5),
        'b_router': normal((DEPTH, N_EXPERTS), 0.01),
        'w_gate_up': normal((DEPTH, N_EXPERTS, D_MODEL, 2 * D_FF), D_MODEL ** -0.5),
        'b_gate_up': normal((DEPTH, N_EXPERTS, 2 * D_FF), 0.02),
        'w_down': normal((DEPTH, N_EXPERTS, D_FF, D_MODEL), D_FF ** -0.5),
        'b_down': normal((DEPTH, N_EXPERTS, D_MODEL), 0.02),
        'g_final': gain((D_MODEL,)),
    }


def reference(x, c, ctx, c_ctx, w_ada, b_ada, g_norm1, g_norm2, w_in,
              s5_lam_re, s5_lam_im, s5_log_step, s5_b_re, s5_b_im, s5_c_re, s5_c_im,
              s5_d, s5_w_glu, s5_b_glu,
              hy_conv_w, hy_conv_b, hy_w1, hy_b1, hy_w2, hy_b2, hy_w3, hy_b3, hy_w4,
              hy_freq, hy_deltas, hy_d,
              w_out, w_router, b_router, w_gate_up, b_gate_up, w_down, b_down, g_final):
    for l in range(DEPTH):
        last = l == DEPTH - 1
        mod_x = jax.nn.silu(c) @ w_ada[l] + b_ada[l]
        mod_c = jax.nn.silu(c_ctx) @ w_ada[l] + b_ada[l]
        sh1, sc1, ga1, sh2, sc2, ga2 = jnp.split(mod_x[:, None, :], N_MOD, axis=-1)
        csh1, csc1, cga1, csh2, csc2, cga2 = jnp.split(mod_c, N_MOD, axis=-1)

        p_lat = modulate(rmsnorm(x, g_norm1[l]), sh1, sc1) @ w_in[l]
        w_in_ctx = w_in[l] if not last else w_in[l][:, :S5_W]
        p_ctx = modulate(rmsnorm(ctx, g_norm1[l]), csh1, csc1) @ w_in_ctx
        y_s5_lat, y_s5_ctx = s5_mixer(
            p_lat[..., :S5_W], p_ctx[..., :S5_W],
            s5_lam_re[l], s5_lam_im[l], s5_log_step[l], s5_b_re[l], s5_b_im[l],
            s5_c_re[l], s5_c_im[l], s5_d[l], s5_w_glu[l], s5_b_glu[l],
            with_ctx_out=not last)
        hy_prm = (hy_conv_w[l], hy_conv_b[l], hy_w1[l], hy_b1[l], hy_w2[l], hy_b2[l],
                  hy_w3[l], hy_b3[l], hy_w4[l], hy_freq[l], hy_deltas[l], hy_d[l])
        y_hy_lat = hyena_mixer(p_lat[..., S5_W:], short_conv_grid, *hy_prm)
        x = x + ga1 * (jnp.concatenate([y_s5_lat, y_hy_lat], axis=-1) @ w_out[l])
        if not last:
            y_hy_ctx = hyena_mixer(p_ctx[..., S5_W:], short_conv_seq, *hy_prm)
            ctx = ctx + cga1 * (jnp.concatenate([y_s5_ctx, y_hy_ctx], axis=-1) @ w_out[l])

        moe_prm = (w_router[l], b_router[l], w_gate_up[l], b_gate_up[l], w_down[l], b_down[l])
        x = x + ga2 * moe_ffn(modulate(rmsnorm(x, g_norm2[l]), sh2, sc2), *moe_prm)
        if not last:
            ctx = ctx + cga2 * moe_ffn(modulate(rmsnorm(ctx, g_norm2[l]), csh2, csc2), *moe_prm)
    return rmsnorm(x, g_final)
```

```python
import functools
import math

import jax
import jax.numpy as jnp
from jax import lax
from jax.experimental import pallas as pl
from jax.experimental.pallas import tpu as pltpu

F32 = jnp.float32
BF16 = jnp.bfloat16
I32 = jnp.int32
U32 = jnp.uint32
HIGHEST = lax.Precision.HIGHEST

GRID_W = 64
S5_H = 16
S5_P = 64
S5_RE_MAX = -1e-4
HY_BANDS = 16
N_EXPERTS = 32
TOP_K = 4
SWIGLU_LIMIT = 7.0
SWIGLU_ALPHA = 1.702
N_MOD = 6
EPS = 1e-6

LANES = 128
SUBLANES = 8
GROUPS_PER_BLOCK = LANES // S5_H
MODES_PER_BLOCK = GROUPS_PER_BLOCK * S5_P
ROW_BLOCK = 512
NEG_BIG = -1e30


def _cparams(sem, vmem_mb):
    return pltpu.CompilerParams(dimension_semantics=sem, vmem_limit_bytes=vmem_mb << 20)


def _dot(a, b):
    return jnp.dot(a, b, preferred_element_type=F32)


def _dot_hi(a, b):
    return jnp.dot(a, b, preferred_element_type=F32, precision=HIGHEST)


def _adaln_body(c_ref, w_ref, b_ref, o_ref):
    c = c_ref[...]
    s = c * jax.nn.sigmoid(c)
    o_ref[...] = _dot_hi(s, w_ref[...]) + b_ref[...]


def _adaln(cc, w, b, tn=1024):
    rows, d = cc.shape
    n = w.shape[1]
    return pl.pallas_call(
        _adaln_body,
        grid=(n // tn,),
        in_specs=[pl.BlockSpec((rows, d), lambda j: (0, 0)),
                  pl.BlockSpec((d, tn), lambda j: (0, j)),
                  pl.BlockSpec((1, tn), lambda j: (0, j))],
        out_specs=pl.BlockSpec((rows, tn), lambda j: (0, j)),
        out_shape=jax.ShapeDtypeStruct((rows, n), F32),
        compiler_params=_cparams(("arbitrary",), 40),
        name="adaln",
    )(cc, w, b)


def _norm_mod(x, g, sh, sc):
    ms = jnp.mean(x * x, axis=-1, keepdims=True)
    return (x * lax.rsqrt(ms + EPS) * g) * (1.0 + sc) + sh


def _inproj_body(x_ref, g_ref, sh_ref, sc_ref, w_ref, o_ref):
    h = _norm_mod(x_ref[0], g_ref[...], sh_ref[0], sc_ref[0])
    o_ref[0] = _dot(h.astype(BF16), w_ref[...])


def _inproj(x, g, sh, sc, w, n_cols, tm, tn=1024):
    b, l, d = x.shape
    if sh.shape[0] > 1:
        bidx = lambda n, bi, li: (bi, 0, 0)
    else:
        bidx = lambda n, bi, li: (0, 0, 0)
    return pl.pallas_call(
        _inproj_body,
        grid=(n_cols // tn, b, l // tm),
        in_specs=[pl.BlockSpec((1, tm, d), lambda n, bi, li: (bi, li, 0)),
                  pl.BlockSpec((1, d), lambda n, bi, li: (0, 0)),
                  pl.BlockSpec((1, 1, d), bidx),
                  pl.BlockSpec((1, 1, d), bidx),
                  pl.BlockSpec((d, tn), lambda n, bi, li: (0, n))],
        out_specs=pl.BlockSpec((1, tm, tn), lambda n, bi, li: (bi, li, n)),
        out_shape=jax.ShapeDtypeStruct((b, l, n_cols), F32),
        compiler_params=_cparams(("arbitrary", "arbitrary", "arbitrary"), 40),
        name="inproj",
    )(x, g, sh, sc, w)


def _s5prep_body(lre_ref, lim_ref, ls_ref, bre_ref, bim_ref, lbr_ref, lbi_ref, bbr_ref, bbi_ref):
    lr = jnp.minimum(lre_ref[...], S5_RE_MAX)
    li = lim_ref[...]
    dt = jnp.exp(ls_ref[...])
    er = jnp.exp(lr * dt)
    lbr = er * jnp.cos(li * dt)
    lbi = er * jnp.sin(li * dt)
    nr = lbr - 1.0
    den = lr * lr + li * li
    qr = (nr * lr + lbi * li) / den
    qi = (lbi * lr - nr * li) / den
    bre = bre_ref[...]
    bim = bim_ref[...]
    lbr_ref[...] = lbr
    lbi_ref[...] = lbi
    bbr_ref[...] = qr * bre - qi * bim
    bbi_ref[...] = qr * bim + qi * bre


def _s5_weights(lam_re, lam_im, log_step, b_re, b_im, c_re, c_im):
    nd, g, p = lam_re.shape
    h = S5_H
    nj = g // GROUPS_PER_BLOCK
    rep = lambda a: jnp.repeat(a.reshape(nd * g, 1, p), h, axis=1).reshape(nd * g * h, p)
    tr = lambda a: jnp.transpose(a, (0, 1, 3, 2)).reshape(nd * g * h, p)
    shp = jax.ShapeDtypeStruct((nd * g * h, p), F32)
    lbr, lbi, bbr, bbi = pl.pallas_call(
        _s5prep_body, out_shape=(shp, shp, shp, shp), name="s5prep",
    )(rep(lam_re), rep(lam_im), rep(log_step), tr(b_re), tr(b_im))
    eye = jnp.eye(GROUPS_PER_BLOCK, dtype=F32)
    bb = jnp.stack([bbr, bbi]).reshape(2, nd, nj, GROUPS_PER_BLOCK, h, p)
    bw = jnp.einsum('rdjahp,ab->djahrbp', bb, eye).reshape(nd, nj, LANES, 2 * MODES_PER_BLOCK)
    cc = jnp.stack([c_re, -c_im]).reshape(2, nd, nj, GROUPS_PER_BLOCK, h, p)
    cw = jnp.einsum('rdjahp,ab->djrapbh', cc, eye).reshape(nd, nj, 2 * MODES_PER_BLOCK, LANES)
    pick = lambda a: a.reshape(nd, g, h, p)[:, :, 0, :].reshape(nd, nj, 1, MODES_PER_BLOCK)
    return bw.astype(BF16), cw.astype(BF16), pick(lbr), pick(lbi)


def _s5_body(*refs, tl, nj, emit_y):
    if emit_y:
        (u_ref, bw_ref, cw_ref, lr_ref, li_ref, init_ref, y_ref, fin_ref, s_scr, bu_scr, st_scr) = refs
    else:
        (u_ref, bw_ref, lr_ref, li_ref, init_ref, fin_ref, s_scr, bu_scr, st_scr) = refs
    d = pl.program_id(0)
    t = pl.program_id(1)
    nt = pl.num_programs(1)
    nb = SUBLANES
    mpb = MODES_PER_BLOCK

    @pl.when(t == 0)
    def _():
        st_scr[...] = init_ref[0]

    for b in range(nb):
        for j in range(nj):
            s_scr[j, pl.ds(b, tl, stride=nb), :] = u_ref[b, :, j * LANES:(j + 1) * LANES]

    for j in range(nj):
        bu_scr[...] = _dot(s_scr[j].astype(BF16), bw_ref[0, j])
        lr = jnp.broadcast_to(lr_ref[0, j], (nb, mpb))
        li = jnp.broadcast_to(li_ref[0, j], (nb, mpb))

        def step(i, carry):
            xr, xi = carry
            l = i + d * (tl - 1 - 2 * i)
            r0 = pl.multiple_of(l * nb, nb)
            br = bu_scr[pl.ds(r0, nb), 0:mpb]
            bi = bu_scr[pl.ds(r0, nb), mpb:2 * mpb]
            nr = lr * xr - li * xi + br
            ni = lr * xi + li * xr + bi
            bu_scr[pl.ds(r0, nb), 0:mpb] = nr
            bu_scr[pl.ds(r0, nb), mpb:2 * mpb] = ni
            return nr, ni

        xr, xi = lax.fori_loop(0, tl, step, (st_scr[j, :, 0:mpb], st_scr[j, :, mpb:2 * mpb]), unroll=4)
        st_scr[j, :, 0:mpb] = xr
        st_scr[j, :, mpb:2 * mpb] = xi
        if emit_y:
            s_scr[j] = _dot(bu_scr[...].astype(BF16), cw_ref[0, j])

    if emit_y:
        for b in range(nb):
            for j in range(nj):
                y_ref[0, b, :, j * LANES:(j + 1) * LANES] = s_scr[j, pl.ds(b, tl, stride=nb), :]

    @pl.when(t == nt - 1)
    def _():
        fin_ref[0] = st_scr[...]


def _s5_scan(u_arr, bw, cw, lamr, lami, init, tl, emit_y):
    b, l = u_arr.shape[:2]
    assert b == SUBLANES
    nj = bw.shape[1]
    c = nj * LANES
    nt = l // tl
    m2 = 2 * MODES_PER_BLOCK
    tmap = lambda d, t: t + d * (nt - 1 - 2 * t)
    wmap = lambda d, t: (d, 0, 0, 0)
    in_specs = [pl.BlockSpec((b, tl, c), lambda d, t: (0, tmap(d, t), 0)),
                pl.BlockSpec((1, nj, LANES, m2), wmap)]
    args = [u_arr, bw]
    if emit_y:
        in_specs.append(pl.BlockSpec((1, nj, m2, LANES), wmap))
        args.append(cw)
    in_specs += [pl.BlockSpec((1, nj, 1, MODES_PER_BLOCK), wmap),
                 pl.BlockSpec((1, nj, 1, MODES_PER_BLOCK), wmap),
                 pl.BlockSpec((1, nj, SUBLANES, m2), wmap)]
    args += [lamr, lami, init]
    fin_spec = pl.BlockSpec((1, nj, SUBLANES, m2), wmap)
    fin_shape = jax.ShapeDtypeStruct((2, nj, SUBLANES, m2), F32)
    if emit_y:
        out_specs = [pl.BlockSpec((1, b, tl, c), lambda d, t: (d, 0, tmap(d, t), 0)), fin_spec]
        out_shape = [jax.ShapeDtypeStruct((2, b, l, c), F32), fin_shape]
    else:
        out_specs = [fin_spec]
        out_shape = [fin_shape]
    return pl.pallas_call(
        functools.partial(_s5_body, tl=tl, nj=nj, emit_y=emit_y),
        grid=(2, nt),
        in_specs=in_specs, out_specs=out_specs, out_shape=out_shape,
        scratch_shapes=[pltpu.VMEM((nj, tl * SUBLANES, LANES), F32),
                        pltpu.VMEM((tl * SUBLANES, m2), F32),
                        pltpu.VMEM((nj, SUBLANES, m2), F32)],
        compiler_params=_cparams(("arbitrary", "arbitrary"), 48),
        name="s5scan_lat" if emit_y else "s5scan_ctx",
    )(*args)


def _hyfilt_body(z_ref, w1_ref, b1_ref, w2_ref, b2_ref, w3_ref, b3_ref, fr_ref,
                 w4f_ref, w4b_ref, df_ref, db_ref, ka_ref, kd_ref, *, length):
    fr = fr_ref[...]
    z = z_ref[...]
    h = jnp.sin(fr * (_dot_hi(z, w1_ref[...]) + b1_ref[...]))
    h = jnp.sin(fr * (_dot_hi(h, w2_ref[...]) + b2_ref[...]))
    h = jnp.sin(fr * (_dot_hi(h, w3_ref[...]) + b3_ref[...]))
    t = z[:, 0:1]
    hf = _dot_hi(h, w4f_ref[...]) * jnp.exp(-t * jnp.abs(df_ref[...]))
    hb = _dot_hi(h, w4b_ref[...]) * jnp.exp(-t * jnp.abs(db_ref[...]))
    row = lax.broadcasted_iota(I32, hf.shape, 0)
    hb = jnp.where(row < length - 1, hb, 0.0)
    norm = jnp.sum(jnp.abs(hf), axis=0, keepdims=True) + jnp.sum(jnp.abs(hb), axis=0, keepdims=True)
    kf = hf / norm
    kb = jnp.where(row >= 1, pltpu.roll(hb, 1, axis=0), 0.0) / norm
    ka_ref[...] = kf + kb
    kd_ref[...] = kf - kb


def _hyena_filter(z, w1, b1, w2, b2, w3, b3, fr, w4, deltas, ct=256):
    length, ze = z.shape
    order = w2.shape[0]
    c = w4.shape[1] // 2
    nct = c // ct
    full = lambda shape: pl.BlockSpec(shape, lambda i: (0, 0))
    shp = jax.ShapeDtypeStruct((length, c), F32)
    return pl.pallas_call(
        functools.partial(_hyfilt_body, length=length),
        grid=(nct,),
        in_specs=[full((length, ze)), full((ze, order)), full((1, order)),
                  full((order, order)), full((1, order)), full((order, order)), full((1, order)),
                  full((1, order)),
                  pl.BlockSpec((order, ct), lambda i: (0, i)),
                  pl.BlockSpec((order, ct), lambda i: (0, nct + i)),
                  pl.BlockSpec((1, ct), lambda i: (0, i)),
                  pl.BlockSpec((1, ct), lambda i: (0, nct + i))],
        out_specs=[pl.BlockSpec((length, ct), lambda i: (0, i))] * 2,
        out_shape=[shp, shp],
        compiler_params=_cparams(("arbitrary",), 48),
        name="hyena_filter",
    )(z, w1, b1, w2, b2, w3, b3, fr, w4, w4, deltas, deltas)


def _dft_tables(length):
    n = 2 * length
    f = jnp.arange(length, dtype=I32)[:, None]
    s = jnp.arange(length, dtype=I32)[None, :]
    ang = (2.0 * math.pi / n) * ((f * s) % n).astype(F32)
    nyq = jnp.where(s % 2 == 0, 1.0, -1.0).astype(F32)
    bot = jnp.where(f == 0, nyq, jnp.sin(ang))
    tab = jnp.concatenate([jnp.cos(ang), bot], axis=0)
    return tab.astype(BF16), tab.T.astype(BF16)


def _hyspec_body(f_ref, ka_ref, kd_ref, kk_ref, kn_ref, *, length, fq):
    q = pl.program_id(1)
    n = 2.0 * length
    a = ka_ref[...]
    rhs = jnp.where(q * fq < length, a, kd_ref[...])
    grow = q * fq + lax.broadcasted_iota(I32, (fq, 1), 0)
    scale = jnp.where(grow == 0, 1.0 / n, jnp.where(grow == length, 0.0, 2.0 / n))
    kk_ref[...] = _dot(f_ref[...], rhs.astype(BF16)) * scale

    @pl.when(q == 0)
    def _():
        row = lax.broadcasted_iota(I32, a.shape, 0)
        sgn = jnp.where(row % 2 == 0, 1.0, -1.0)
        kn = jnp.sum(a * sgn, axis=0, keepdims=True) * (1.0 / n)
        kn_ref[...] = jnp.broadcast_to(kn, kn_ref.shape)


def _hyena_spectrum(ftab, ka, kd, ct=256, fq=512):
    length, c = ka.shape
    kshape = jax.ShapeDtypeStruct((2 * length, c), F32)
    return pl.pallas_call(
        functools.partial(_hyspec_body, length=length, fq=fq),
        grid=(c // ct, 2 * length // fq),
        in_specs=[pl.BlockSpec((fq, length), lambda i, q: (q, 0)),
                  pl.BlockSpec((length, ct), lambda i, q: (0, i)),
                  pl.BlockSpec((length, ct), lambda i, q: (0, i))],
        out_specs=[pl.BlockSpec((fq, ct), lambda i, q: (q, i)),
                   pl.BlockSpec((SUBLANES, ct), lambda i, q: (0, i))],
        out_shape=[kshape, jax.ShapeDtypeStruct((SUBLANES, c), F32)],
        compiler_params=_cparams(("arbitrary", "arbitrary"), 48),
        name="hyena_spectrum",
    )(ftab, ka, kd)


def _hypre_body(pv_ref, p1_ref, p0_ref, wv_ref, w1_ref, w0_ref, bv_ref, b1_ref, b0_ref, v_ref, x0_ref):
    tm = pv_ref.shape[1]
    row = lax.broadcasted_iota(I32, (tm, 1), 0) % GRID_W

    def sconv(p_ref, w_ref, b_ref):
        a = p_ref[0]
        prev = jnp.where(row == 0, 0.0, pltpu.roll(a, 1, axis=0))
        nxt = jnp.where(row == GRID_W - 1, 0.0, pltpu.roll(a, tm - 1, axis=0))
        return prev * w_ref[0:1, :] + a * w_ref[1:2, :] + nxt * w_ref[2:3, :] + b_ref[...]

    v_ref[0] = (sconv(pv_ref, wv_ref, bv_ref) * sconv(p1_ref, w1_ref, b1_ref)).astype(v_ref.dtype)
    x0_ref[0] = sconv(p0_ref, w0_ref, b0_ref).astype(x0_ref.dtype)


def _hyena_pre(p, conv_w, conv_b, c, first_block, tm=512):
    b, l, _ = p.shape
    pspec = lambda k: pl.BlockSpec((1, tm, c), lambda bi, li: (bi, li, first_block + k))
    wspec = lambda k: pl.BlockSpec((3, c), lambda bi, li: (0, k))
    bspec = lambda k: pl.BlockSpec((1, c), lambda bi, li: (0, k))
    shp = jax.ShapeDtypeStruct((b, l, c), BF16)
    return pl.pallas_call(
        _hypre_body,
        grid=(b, l // tm),
        in_specs=[pspec(0), pspec(1), pspec(2), wspec(0), wspec(1), wspec(2), bspec(0), bspec(1), bspec(2)],
        out_specs=[pl.BlockSpec((1, tm, c), lambda bi, li: (bi, li, 0))] * 2,
        out_shape=[shp, shp],
        compiler_params=_cparams(("arbitrary", "arbitrary"), 48),
        name="hyena_pre",
    )(p, p, p, conv_w, conv_w, conv_w, conv_b, conv_b, conv_b)


def _hyconv_body(v_ref, x0_ref, kk_ref, kn_ref, d_ref, f_hbm, g_hbm, o_ref, f_scr, g_scr, sem, *, length, fq):
    first = jnp.logical_and(pl.program_id(0) == 0, pl.program_id(1) == 0)

    @pl.when(first)
    def _():
        cf = pltpu.make_async_copy(f_hbm, f_scr, sem.at[0])
        cg = pltpu.make_async_copy(g_hbm, g_scr, sem.at[1])
        cf.start()
        cg.start()
        cf.wait()
        cg.wait()

    vb = v_ref[0]
    acc = jnp.zeros((length, vb.shape[1]), F32)
    for q in range(length // fq):
        lo, hi = q * fq, (q + 1) * fq
        vr = _dot(f_scr[lo:hi, :], vb)
        vs = _dot(f_scr[length + lo:length + hi, :], vb)
        kr = kk_ref[lo:hi, :]
        ks = kk_ref[length + lo:length + hi, :]
        if q == 0:
            row0 = lax.broadcasted_iota(I32, (fq, 1), 0) == 0
            vn = vs[0:1, :]
            vs = jnp.where(row0, 0.0, vs)
        pr = vr * kr - vs * ks
        ps = vr * ks + vs * kr
        if q == 0:
            ps = jnp.where(row0, vn * kn_ref[0:1, :], ps)
        acc = acc + _dot(g_scr[:, lo:hi], pr.astype(BF16)) + _dot(g_scr[:, length + lo:length + hi], ps.astype(BF16))
    v = vb.astype(F32)
    o_ref[0] = ((acc + v * d_ref[...]) * x0_ref[0].astype(F32)).astype(o_ref.dtype)


def _hyena_conv(v, x0, kk, kn, d_skip, ftab, gtab, ct=256, fq=512):
    b, l, c = v.shape
    return pl.pallas_call(
        functools.partial(_hyconv_body, length=l, fq=fq),
        grid=(c // ct, b),
        in_specs=[pl.BlockSpec((1, l, ct), lambda ci, bi: (bi, 0, ci)),
                  pl.BlockSpec((1, l, ct), lambda ci, bi: (bi, 0, ci)),
                  pl.BlockSpec((2 * l, ct), lambda ci, bi: (0, ci)),
                  pl.BlockSpec((SUBLANES, ct), lambda ci, bi: (0, ci)),
                  pl.BlockSpec((1, ct), lambda ci, bi: (0, ci)),
                  pl.BlockSpec(memory_space=pl.ANY),
                  pl.BlockSpec(memory_space=pl.ANY)],
        out_specs=pl.BlockSpec((1, l, ct), lambda ci, bi: (bi, 0, ci)),
        out_shape=jax.ShapeDtypeStruct((b, l, c), BF16),
        scratch_shapes=[pltpu.VMEM((2 * l, l), BF16), pltpu.VMEM((l, 2 * l), BF16),
                        pltpu.SemaphoreType.DMA((2,))],
        compiler_params=_cparams(("arbitrary", "arbitrary"), 56),
        name="hyena_conv",
    )(v, x0, kk, kn, d_skip, ftab, gtab)


def _pack_pairs(h):
    half = h.shape[1] // 2
    lo = lax.bitcast_convert_type(h[:, :half].astype(BF16).astype(F32), U32) >> 16
    hi = lax.bitcast_convert_type(h[:, half:].astype(BF16).astype(F32), U32) & jnp.uint32(0xFFFF0000)
    return lo | hi


def _unpack_pairs(w):
    lo = lax.bitcast_convert_type(w << 16, F32).astype(BF16)
    hi = lax.bitcast_convert_type(w & jnp.uint32(0xFFFF0000), F32).astype(BF16)
    return lo, hi


def _outproj_body(y_ref, u_ref, yh_ref, x_ref, d_ref, wg_ref, bg_ref, wo_ref, ga_ref, g2_ref,
                  sh_ref, sc_ref, wr_ref, br_ref, x1_ref, hp_ref, te_ref, gt_ref, cnt_ref):
    c = u_ref.shape[2]
    ys = y_ref[0, 0] + y_ref[1, 0] + u_ref[0] * d_ref[...]
    ge = 0.5 * ys * (1.0 + lax.erf(ys * (1.0 / math.sqrt(2.0))))
    gl = ge * jax.nn.sigmoid(_dot(ge.astype(BF16), wg_ref[...]) + bg_ref[...])
    mix = _dot(gl.astype(BF16), wo_ref[0:c, :]) + _dot(yh_ref[0], wo_ref[c:2 * c, :])
    x1 = x_ref[0] + ga_ref[0] * mix
    x1_ref[0] = x1
    h2 = _norm_mod(x1, g2_ref[...], sh_ref[0], sc_ref[0])
    hp_ref[0] = _pack_pairs(h2)

    logits = _dot_hi(h2, wr_ref[...]) + br_ref[...]
    tm = logits.shape[0]
    lane = lax.broadcasted_iota(I32, (tm, LANES), 1)
    work = logits
    vals, idxs = [], []
    for _ in range(TOP_K):
        m = jnp.max(work, axis=-1, keepdims=True)
        ix = jnp.min(jnp.where(work == m, lane, LANES), axis=-1, keepdims=True)
        vals.append(m)
        idxs.append(ix)
        work = jnp.where(lane == ix, NEG_BIG, work)
    ex = [jnp.exp(v - vals[0]) for v in vals]
    den = ex[0]
    for e in ex[1:]:
        den = den + e
    te = jnp.zeros((tm, LANES), I32)
    gt = jnp.zeros((tm, LANES), F32)
    mh = jnp.zeros((tm, LANES), F32)
    for k in range(TOP_K):
        te = jnp.where(lane == k, idxs[k], te)
        gt = jnp.where(lane == k, ex[k] / den, gt)
        mh = mh + (lane == idxs[k]).astype(F32)
    te_ref[0] = te
    gt_ref[0] = gt

    @pl.when(jnp.logical_and(pl.program_id(0) == 0, pl.program_id(1) == 0))
    def _():
        cnt_ref[...] = jnp.zeros_like(cnt_ref)

    cnt_ref[...] += jnp.broadcast_to(jnp.sum(mh, axis=0, keepdims=True), cnt_ref.shape)


def _outproj(y, p, yh, x, d_skip, w_glu, b_glu, w_out, ga1, g2, sh2, sc2, w_r, b_r, tm=256):
    b, l, dm = x.shape
    c = yh.shape[2]
    tok = lambda width: pl.BlockSpec((1, tm, width), lambda bi, li: (bi, li, 0))
    full = lambda shape: pl.BlockSpec(shape, lambda bi, li: (0,) * len(shape))
    perb = pl.BlockSpec((1, 1, dm), lambda bi, li: (bi, 0, 0))
    return pl.pallas_call(
        _outproj_body,
        grid=(b, l // tm),
        in_specs=[pl.BlockSpec((2, 1, tm, c), lambda bi, li: (0, bi, li, 0)),
                  tok(c), tok(c), tok(dm),
                  full((1, c)), full((c, c)), full((1, c)), full((2 * c, dm)),
                  perb, full((1, dm)), perb, perb,
                  full((dm, LANES)), full((1, LANES))],
        out_specs=[tok(dm), tok(dm // 2), tok(LANES), tok(LANES), full((SUBLANES, LANES))],
        out_shape=[jax.ShapeDtypeStruct((b, l, dm), F32),
                   jax.ShapeDtypeStruct((b, l, dm // 2), U32),
                   jax.ShapeDtypeStruct((b, l, LANES), I32),
                   jax.ShapeDtypeStruct((b, l, LANES), F32),
                   jax.ShapeDtypeStruct((SUBLANES, LANES), F32)],
        compiler_params=_cparams(("arbitrary", "arbitrary"), 56),
        name="outproj_router",
    )(y, p, yh, x, d_skip, w_glu, b_glu, w_out, ga1, g2, sh2, sc2, w_r, b_r)


def _slots_body(te_ref, seg_ref, pos_ref, carry):
    @pl.when(pl.program_id(0) == 0)
    def _():
        carry[...] = jnp.zeros_like(carry)

    te = te_ref[...]
    tm = te.shape[0]
    lane = lax.broadcasted_iota(I32, (tm, LANES), 1)
    sel = [lane == te[:, k:k + 1] for k in range(TOP_K)]
    mh = jnp.zeros((tm, LANES), F32)
    for s in sel:
        mh = mh + s.astype(F32)
    r = lax.broadcasted_iota(I32, (tm, tm), 0)
    cc = lax.broadcasted_iota(I32, (tm, tm), 1)
    strict_lower = (r > cc).astype(BF16)
    slot = _dot(strict_lower, mh.astype(BF16)) + carry[0:1, :] + seg_ref[0:1, :]
    pos = jnp.zeros((tm, LANES), F32)
    for k in range(TOP_K):
        pk = jnp.sum(jnp.where(sel[k], slot, 0.0), axis=-1, keepdims=True)
        pos = jnp.where(lane == k, pk, pos)
    pos_ref[...] = pos.astype(I32)
    carry[...] += jnp.broadcast_to(jnp.sum(mh, axis=0, keepdims=True), carry.shape)


def _slots(te, seg_start, tm=512):
    t = te.shape[0]
    return pl.pallas_call(
        _slots_body,
        grid=(t // tm,),
        in_specs=[pl.BlockSpec((tm, LANES), lambda i: (i, 0)),
                  pl.BlockSpec((SUBLANES, LANES), lambda i: (0, 0))],
        out_specs=pl.BlockSpec((tm, LANES), lambda i: (i, 0)),
        out_shape=jax.ShapeDtypeStruct((t, LANES), I32),
        scratch_shapes=[pltpu.VMEM((SUBLANES, LANES), F32)],
        compiler_params=_cparams(("arbitrary",), 32),
        name="moe_slots",
    )(te, seg_start)


def _row_copy(src, src_row, dst, dst_row, sem):
    return pltpu.make_async_copy(src.at[pl.ds(src_row, 1)], dst.at[pl.ds(dst_row, 1)], sem)


def _dispatch_body(pos_ref, h_ref, xs_in, xs_out, sem, *, tm):
    del xs_in
    base = pl.program_id(0) * (tm * TOP_K)

    def issue(r, carry):
        for k in range(TOP_K):
            _row_copy(h_ref, r, xs_out, pos_ref[base + r * TOP_K + k], sem).start()
        return carry

    lax.fori_loop(0, tm, issue, 0, unroll=8)

    def drain(r, carry):
        for k in range(TOP_K):
            _row_copy(h_ref, 0, xs_out, 0, sem).wait()
        return carry

    lax.fori_loop(0, tm, drain, 0, unroll=8)


def _dispatch(pos_flat, hp, xs_init, tm=256):
    t, w = hp.shape
    return pl.pallas_call(
        functools.partial(_dispatch_body, tm=tm),
        grid_spec=pltpu.PrefetchScalarGridSpec(
            num_scalar_prefetch=1,
            grid=(t // tm,),
            in_specs=[pl.BlockSpec((tm, w), lambda i, pos: (i, 0)),
                      pl.BlockSpec(memory_space=pl.ANY)],
            out_specs=pl.BlockSpec(memory_space=pl.ANY),
            scratch_shapes=[pltpu.SemaphoreType.DMA(())]),
        out_shape=jax.ShapeDtypeStruct(xs_init.shape, xs_init.dtype),
        input_output_aliases={2: 0},
        compiler_params=_cparams(("arbitrary",), 32),
        name="moe_dispatch",
    )(pos_flat, hp, xs_init)


def _expert_changed(be_ref, i):
    prev = be_ref[jnp.maximum(i - 1, 0)]
    return jnp.logical_or(i == 0, be_ref[i] != prev)


def _gateup_body(be_ref, nu_ref, x_ref, wg_ref, wu_ref, bg_ref, bu_ref, o_ref, wg_s, wu_s):
    i = pl.program_id(1)

    @pl.when(_expert_changed(be_ref, i))
    def _():
        wg_s[...] = wg_ref[...].astype(BF16)
        wu_s[...] = wu_ref[...].astype(BF16)

    @pl.when(i < nu_ref[0])
    def _():
        lo, hi = _unpack_pairs(x_ref[...])
        half = lo.shape[1]
        g = _dot(lo, wg_s[0:half, :]) + _dot(hi, wg_s[half:2 * half, :]) + bg_ref[...]
        u = _dot(lo, wu_s[0:half, :]) + _dot(hi, wu_s[half:2 * half, :]) + bu_ref[...]
        g = jnp.minimum(g, SWIGLU_LIMIT)
        u = jnp.clip(u, -SWIGLU_LIMIT, SWIGLU_LIMIT)
        o_ref[...] = ((u + 1.0) * (g * jax.nn.sigmoid(SWIGLU_ALPHA * g))).astype(o_ref.dtype)

    @pl.when(i >= nu_ref[0])
    def _():
        o_ref[...] = jnp.zeros_like(o_ref)


def _expert_gateup(blk_e, n_used, xs, w_gate_up, b_gate_up, tn=512):
    e, dm, ff2 = w_gate_up.shape
    ff = ff2 // 2
    nr = xs.shape[0]
    nb = nr // ROW_BLOCK
    nn = ff // tn
    last = lambda i, nu: jnp.minimum(i, nu[0] - 1)
    return pl.pallas_call(
        _gateup_body,
        grid_spec=pltpu.PrefetchScalarGridSpec(
            num_scalar_prefetch=2,
            grid=(nn, nb),
            in_specs=[pl.BlockSpec((ROW_BLOCK, dm // 2), lambda j, i, be, nu: (last(i, nu), 0)),
                      pl.BlockSpec((None, dm, tn), lambda j, i, be, nu: (be[i], 0, j)),
                      pl.BlockSpec((None, dm, tn), lambda j, i, be, nu: (be[i], 0, nn + j)),
                      pl.BlockSpec((None, 1, tn), lambda j, i, be, nu: (be[i], 0, j)),
                      pl.BlockSpec((None, 1, tn), lambda j, i, be, nu: (be[i], 0, nn + j))],
            out_specs=pl.BlockSpec((ROW_BLOCK, tn), lambda j, i, be, nu: (i, j)),
            scratch_shapes=[pltpu.VMEM((dm, tn), BF16), pltpu.VMEM((dm, tn), BF16)]),
        out_shape=jax.ShapeDtypeStruct((nr, ff), BF16),
        compiler_params=_cparams(("arbitrary", "arbitrary"), 48),
        name="moe_gateup",
    )(blk_e, n_used, xs, w_gate_up, w_gate_up, b_gate_up, b_gate_up)


def _down_body(be_ref, nu_ref, h_ref, w_ref, b_ref, o_ref, w_s):
    i = pl.program_id(1)

    @pl.when(_expert_changed(be_ref, i))
    def _():
        w_s[...] = w_ref[...].astype(BF16)

    @pl.when(i < nu_ref[0])
    def _():
        o_ref[...] = _dot(h_ref[...], w_s[...]) + b_ref[...]

    @pl.when(i >= nu_ref[0])
    def _():
        o_ref[...] = jnp.zeros_like(o_ref)


def _expert_down(blk_e, n_used, h, w_down, b_down, tn=1024):
    e, ff, dm = w_down.shape
    nr = h.shape[0]
    nb = nr // ROW_BLOCK
    last = lambda i, nu: jnp.minimum(i, nu[0] - 1)
    return pl.pallas_call(
        _down_body,
        grid_spec=pltpu.PrefetchScalarGridSpec(
            num_scalar_prefetch=2,
            grid=(dm // tn, nb),
            in_specs=[pl.BlockSpec((ROW_BLOCK, ff), lambda j, i, be, nu: (last(i, nu), 0)),
                      pl.BlockSpec((None, ff, tn), lambda j, i, be, nu: (be[i], 0, j)),
                      pl.BlockSpec((None, 1, tn), lambda j, i, be, nu: (be[i], 0, j))],
            out_specs=pl.BlockSpec((ROW_BLOCK, tn), lambda j, i, be, nu: (i, j)),
            scratch_shapes=[pltpu.VMEM((ff, tn), BF16)]),
        out_shape=jax.ShapeDtypeStruct((nr, dm), F32),
        compiler_params=_cparams(("arbitrary", "arbitrary"), 48),
        name="moe_down",
    )(blk_e, n_used, h, w_down, b_down)


def _combine_body(pos_ref, x1_ref, gt_ref, ga_ref, gf_ref, y_hbm, o_ref, ybuf, sem, *, tm):
    i = pl.program_id(0)
    n = pl.num_programs(0)

    def gather(tile, slot):
        def issue(r, carry):
            for k in range(TOP_K):
                p = pos_ref[(tile * tm + r) * TOP_K + k]
                _row_copy(y_hbm, p, ybuf.at[slot, k], r, sem.at[slot]).start()
            return carry
        lax.fori_loop(0, tm, issue, 0, unroll=8)

    @pl.when(i == 0)
    def _():
        gather(0, 0)

    @pl.when(i + 1 < n)
    def _():
        gather(i + 1, (i + 1) % 2)

    slot = i % 2

    def drain(r, carry):
        for k in range(TOP_K):
            _row_copy(y_hbm, 0, ybuf.at[slot, k], 0, sem.at[slot]).wait()
        return carry

    lax.fori_loop(0, tm, drain, 0, unroll=8)

    gt = gt_ref[0]
    acc = gt[:, 0:1] * ybuf[slot, 0]
    for k in range(1, TOP_K):
        acc = acc + gt[:, k:k + 1] * ybuf[slot, k]
    x2 = x1_ref[0] + ga_ref[0] * acc
    ms = jnp.mean(x2 * x2, axis=-1, keepdims=True)
    o_ref[0] = x2 * lax.rsqrt(ms + EPS) * gf_ref[...]


def _combine(pos_flat, x1, gate, ga2, g_final, y, tm=128):
    b, l, dm = x1.shape
    per = l // tm
    tok = lambda width: pl.BlockSpec((1, tm, width), lambda i, pos: (i // per, i % per, 0))
    return pl.pallas_call(
        functools.partial(_combine_body, tm=tm),
        grid_spec=pltpu.PrefetchScalarGridSpec(
            num_scalar_prefetch=1,
            grid=(b * per,),
            in_specs=[tok(dm), tok(LANES),
                      pl.BlockSpec((1, 1, dm), lambda i, pos: (i // per, 0, 0)),
                      pl.BlockSpec((1, dm), lambda i, pos: (0, 0)),
                      pl.BlockSpec(memory_space=pl.ANY)],
            out_specs=tok(dm),
            scratch_shapes=[pltpu.VMEM((2, TOP_K, tm, dm), F32), pltpu.SemaphoreType.DMA((2,))]),
        out_shape=jax.ShapeDtypeStruct((b, l, dm), F32),
        compiler_params=_cparams(("arbitrary",), 48),
        name="moe_combine",
    )(pos_flat, x1, gate, ga2, g_final, y)


def _filter_features(length):
    t = jnp.linspace(0.0, 1.0, length, dtype=F32)[:, None]
    bands = jnp.linspace(1e-4, HY_BANDS - 1, HY_BANDS, dtype=F32)
    ang = (2 * math.pi / length) * jnp.arange(length, dtype=F32)[:, None] * bands
    z = jnp.concatenate([t, jnp.cos(ang), -jnp.sin(ang)], axis=-1)
    return jnp.pad(z, ((0, 0), (0, LANES - z.shape[1])))


def _segments(counts, n_blocks):
    padded = (counts + ROW_BLOCK - 1) // ROW_BLOCK * ROW_BLOCK
    seg_end = jnp.cumsum(padded)
    seg_start = seg_end - padded
    n_used = (seg_end[-1] // ROW_BLOCK).astype(I32)
    blk = jnp.minimum(jnp.arange(n_blocks, dtype=I32), n_used - 1)
    blk_e = jnp.searchsorted(seg_end, blk * ROW_BLOCK, side='right').astype(I32)
    return seg_start, blk_e, n_used.reshape(1)


def kernel(x, c, ctx, c_ctx, w_ada, b_ada, g_norm1, g_norm2, w_in, s5_lam_re, s5_lam_im, s5_log_step,
           s5_b_re, s5_b_im, s5_c_re, s5_c_im, s5_d, s5_w_glu, s5_b_glu, hy_conv_w, hy_conv_b,
           hy_w1, hy_b1, hy_w2, hy_b2, hy_w3, hy_b3, hy_w4, hy_freq, hy_deltas, hy_d, w_out,
           w_router, b_router, w_gate_up, b_gate_up, w_down, b_down, g_final):
    assert w_ada.shape[0] == 1, "single-layer block"
    b, l, dm = x.shape
    lc = ctx.shape[1]
    s5w = s5_d.shape[1]
    hyw = hy_d.shape[1]
    n_exp = w_router.shape[2]
    row = lambda a: a.reshape(1, -1)

    cc = jnp.concatenate([c, c_ctx[None], jnp.zeros((2 * SUBLANES - b - 1, dm), F32)], axis=0)
    mod = _adaln(cc, w_ada[0], row(b_ada[0]))
    sh1, sc1, ga1, sh2, sc2, ga2 = [mod[:b, k * dm:(k + 1) * dm].reshape(b, 1, dm) for k in range(N_MOD)]
    csh1, csc1 = [mod[b:b + 1, k * dm:(k + 1) * dm].reshape(1, 1, dm) for k in range(2)]

    w_in_b = w_in[0].astype(BF16)
    g1 = row(g_norm1[0])
    p = _inproj(x, g1, sh1, sc1, w_in_b, w_in_b.shape[1], tm=512)
    p_ctx = _inproj(ctx, g1, csh1, csc1, w_in_b, s5w, tm=lc)

    bw, cw, lamr, lami = _s5_weights(s5_lam_re[0], s5_lam_im[0], s5_log_step[0],
                                     s5_b_re[0], s5_b_im[0], s5_c_re[0], s5_c_im[0])
    zero_state = jnp.zeros((2, bw.shape[1], SUBLANES, 2 * MODES_PER_BLOCK), F32)
    (ctx_state,) = _s5_scan(p_ctx, bw, cw, lamr, lami, zero_state, tl=64, emit_y=False)
    y_s5, _ = _s5_scan(p, bw, cw, lamr, lami, ctx_state, tl=64, emit_y=True)

    ka, kd = _hyena_filter(_filter_features(l), jnp.pad(hy_w1[0], ((0, LANES - hy_w1.shape[1]), (0, 0))),
                           row(hy_b1[0]), hy_w2[0], row(hy_b2[0]), hy_w3[0], row(hy_b3[0]),
                           row(hy_freq[0]), hy_w4[0], row(hy_deltas[0]))
    ftab, gtab = _dft_tables(l)
    kk, kn = _hyena_spectrum(ftab, ka, kd)
    v, x0 = _hyena_pre(p, hy_conv_w[0], row(hy_conv_b[0]), hyw, s5w // hyw)
    y_hy = _hyena_conv(v, x0, kk, kn, row(hy_d[0]), ftab, gtab)

    w_r = jnp.pad(w_router[0], ((0, 0), (0, LANES - n_exp)))
    b_r = jnp.concatenate([b_router[0], jnp.full((LANES - n_exp,), NEG_BIG, F32)]).reshape(1, LANES)
    x1, hp, te, gate, cnt = _outproj(y_s5, p, y_hy, x, row(s5_d[0]), s5_w_glu[0].astype(BF16),
                                     row(s5_b_glu[0]), w_out[0].astype(BF16), ga1, row(g_norm2[0]),
                                     sh2, sc2, w_r, b_r)

    t = b * l
    n_blocks = t * TOP_K // ROW_BLOCK + n_exp
    counts = cnt[0, :n_exp].astype(I32)
    seg_start, blk_e, n_used = _segments(counts, n_blocks)
    seg_row = jnp.broadcast_to(jnp.pad(seg_start.astype(F32), (0, LANES - n_exp)), (SUBLANES, LANES))
    pos = _slots(te.reshape(t, LANES), seg_row)
    pos_flat = pos[:, :TOP_K].reshape(-1)
    xs = _dispatch(pos_flat, hp.reshape(t, dm // 2), jnp.zeros((n_blocks * ROW_BLOCK, dm // 2), U32))
    hid = _expert_gateup(blk_e, n_used, xs, w_gate_up[0], b_gate_up[0].reshape(n_exp, 1, -1))
    y_e = _expert_down(blk_e, n_used, hid, w_down[0], b_down[0].reshape(n_exp, 1, -1))
    return _combine(pos_flat, x1, gate, ga2, row(g_final), y_e)
```

```python
import functools
import math

import jax
import jax.numpy as jnp
from jax import lax
from jax.experimental import pallas as pl
from jax.experimental.pallas import tpu as pltpu

F32 = jnp.float32
BF16 = jnp.bfloat16
I32 = jnp.int32
U32 = jnp.uint32
HIGHEST = lax.Precision.HIGHEST

GRID_W = 64
S5_H = 16
S5_P = 64
S5_RE_MAX = -1e-4
HY_BANDS = 16
N_EXPERTS = 32
TOP_K = 4
SWIGLU_LIMIT = 7.0
SWIGLU_ALPHA = 1.702
N_MOD = 6
EPS = 1e-6

LANES = 128
SUBLANES = 8
GROUPS_PER_BLOCK = LANES // S5_H
MODES_PER_BLOCK = GROUPS_PER_BLOCK * S5_P
ROW_BLOCK = 256
NEG_BIG = -1e30


def _cparams(sem, vmem_mb):
    return pltpu.CompilerParams(dimension_semantics=sem, vmem_limit_bytes=vmem_mb << 20)


def _dot(a, b):
    return jnp.dot(a, b, preferred_element_type=F32)


def _dot_hi(a, b):
    return jnp.dot(a, b, preferred_element_type=F32, precision=HIGHEST)


def _adaln_body(c_ref, w_ref, b_ref, o_ref):
    c = c_ref[...]
    s = c * jax.nn.sigmoid(c)
    o_ref[...] = _dot_hi(s, w_ref[...]) + b_ref[...]


def _adaln(cc, w, b, tn=1024):
    rows, d = cc.shape
    n = w.shape[1]
    return pl.pallas_call(
        _adaln_body,
        grid=(n // tn,),
        in_specs=[pl.BlockSpec((rows, d), lambda j: (0, 0)),
                  pl.BlockSpec((d, tn), lambda j: (0, j)),
                  pl.BlockSpec((1, tn), lambda j: (0, j))],
        out_specs=pl.BlockSpec((rows, tn), lambda j: (0, j)),
        out_shape=jax.ShapeDtypeStruct((rows, n), F32),
        compiler_params=_cparams(("arbitrary",), 40),
        name="adaln",
    )(cc, w, b)


def _norm_mod(x, g, sh, sc):
    ms = jnp.mean(x * x, axis=-1, keepdims=True)
    return (x * lax.rsqrt(ms + EPS) * g) * (1.0 + sc) + sh


def _inproj_body(x_ref, g_ref, sh_ref, sc_ref, w_ref, o_ref):
    half = x_ref.shape[1] // 2
    for r0 in (0, half):
        rs = slice(r0, r0 + half)
        h = _norm_mod(x_ref[0, rs, :], g_ref[...], sh_ref[0], sc_ref[0])
        o_ref[0, rs, :] = _dot(h.astype(BF16), w_ref[...])


def _inproj(x, g, sh, sc, w, n_cols, tm, tn=1024):
    b, l, d = x.shape
    if sh.shape[0] > 1:
        bidx = lambda n, bi, li: (bi, 0, 0)
    else:
        bidx = lambda n, bi, li: (0, 0, 0)
    return pl.pallas_call(
        _inproj_body,
        grid=(n_cols // tn, b, l // tm),
        in_specs=[pl.BlockSpec((1, tm, d), lambda n, bi, li: (bi, li, 0)),
                  pl.BlockSpec((1, d), lambda n, bi, li: (0, 0)),
                  pl.BlockSpec((1, 1, d), bidx),
                  pl.BlockSpec((1, 1, d), bidx),
                  pl.BlockSpec((d, tn), lambda n, bi, li: (0, n))],
        out_specs=pl.BlockSpec((1, tm, tn), lambda n, bi, li: (bi, li, n)),
        out_shape=jax.ShapeDtypeStruct((b, l, n_cols), F32),
        compiler_params=_cparams(("arbitrary", "arbitrary", "arbitrary"), 40),
        name="inproj",
    )(x, g, sh, sc, w)


def _s5prep_body(lre_ref, lim_ref, ls_ref, bre_ref, bim_ref, lbr_ref, lbi_ref, bbr_ref, bbi_ref):
    lr = jnp.minimum(lre_ref[...], S5_RE_MAX)
    li = lim_ref[...]
    dt = jnp.exp(ls_ref[...])
    er = jnp.exp(lr * dt)
    lbr = er * jnp.cos(li * dt)
    lbi = er * jnp.sin(li * dt)
    nr = lbr - 1.0
    den = lr * lr + li * li
    qr = (nr * lr + lbi * li) / den
    qi = (lbi * lr - nr * li) / den
    bre = bre_ref[...]
    bim = bim_ref[...]
    lbr_ref[...] = lbr
    lbi_ref[...] = lbi
    bbr_ref[...] = qr * bre - qi * bim
    bbi_ref[...] = qr * bim + qi * bre


def _s5_weights(lam_re, lam_im, log_step, b_re, b_im, c_re, c_im):
    nd, g, p = lam_re.shape
    h = S5_H
    nj = g // GROUPS_PER_BLOCK
    rep = lambda a: jnp.repeat(a.reshape(nd * g, 1, p), h, axis=1).reshape(nd * g * h, p)
    tr = lambda a: jnp.transpose(a, (0, 1, 3, 2)).reshape(nd * g * h, p)
    shp = jax.ShapeDtypeStruct((nd * g * h, p), F32)
    lbr, lbi, bbr, bbi = pl.pallas_call(
        _s5prep_body, out_shape=(shp, shp, shp, shp), name="s5prep",
    )(rep(lam_re), rep(lam_im), rep(log_step), tr(b_re), tr(b_im))
    eye = jnp.eye(GROUPS_PER_BLOCK, dtype=F32)
    bb = jnp.stack([bbr, bbi]).reshape(2, nd, nj, GROUPS_PER_BLOCK, h, p)
    bw = jnp.einsum('rdjahp,ab->djahrbp', bb, eye).reshape(nd, nj, LANES, 2 * MODES_PER_BLOCK)
    cc = jnp.stack([c_re, -c_im]).reshape(2, nd, nj, GROUPS_PER_BLOCK, h, p)
    cw = jnp.einsum('rdjahp,ab->djrapbh', cc, eye).reshape(nd, nj, 2 * MODES_PER_BLOCK, LANES)
    pick = lambda a: a.reshape(nd, g, h, p)[:, :, 0, :].reshape(nd, nj, 1, MODES_PER_BLOCK)
    return bw.astype(BF16), cw.astype(BF16), pick(lbr), pick(lbi)


def _s5_body(*refs, tl, nt, nj, emit_y):
    if emit_y:
        (u_hbm, bw_ref, cw_ref, lr_ref, li_ref, init_ref, y_hbm, fin_ref,
         u_scr, bu_scr, st_scr, in_sem, y_scr, out_sem) = refs
    else:
        (u_hbm, bw_ref, lr_ref, li_ref, init_ref, fin_ref, u_scr, bu_scr, st_scr, in_sem) = refs
    g = pl.program_id(0)
    ng = pl.num_programs(0)
    d = g // nt
    t = g % nt
    slot = g % 2
    nb = SUBLANES
    mpb = MODES_PER_BLOCK
    c = nj * LANES
    rows = tl * nb

    def tile_of(step):
        dd = step // nt
        tt = step % nt
        return dd, tt + dd * (nt - 1 - 2 * tt)

    def in_copies(step, sl):
        _, tile = tile_of(step)
        return [pltpu.make_async_copy(u_hbm.at[b, pl.ds(tile * tl, tl), pl.ds(0, c)],
                                      u_scr.at[sl, :, b, :], in_sem.at[sl]) for b in range(nb)]

    def out_copies(step, sl):
        dd, tile = tile_of(step)
        return [pltpu.make_async_copy(y_scr.at[sl, :, b, :], y_hbm.at[dd, b, pl.ds(tile * tl, tl), :],
                                      out_sem.at[sl]) for b in range(nb)]

    @pl.when(g == 0)
    def _():
        for cp in in_copies(0, 0):
            cp.start()

    @pl.when(g + 1 < ng)
    def _():
        for cp in in_copies(g + 1, (g + 1) % 2):
            cp.start()

    for cp in in_copies(g, slot):
        cp.wait()

    if emit_y:
        @pl.when(g >= 2)
        def _():
            for cp in out_copies(g - 2, slot):
                cp.wait()

    @pl.when(t == 0)
    def _():
        st_scr[...] = init_ref[0]

    def project(j):
        s = u_scr[slot, :, :, j * LANES:(j + 1) * LANES].reshape(rows, LANES)
        bu_scr[j % 3] = _dot(s.astype(BF16), bw_ref[0, j])

    def readout(j):
        y = _dot(bu_scr[j % 3].astype(BF16), cw_ref[0, j])
        y_scr[slot, :, :, j * LANES:(j + 1) * LANES] = y.reshape(tl, nb, LANES)

    project(0)
    for j in range(nj):
        if j + 1 < nj:
            project(j + 1)
        if emit_y and j >= 1:
            readout(j - 1)
        lr = jnp.broadcast_to(lr_ref[0, j], (nb, mpb))
        li = jnp.broadcast_to(li_ref[0, j], (nb, mpb))
        cur = j % 3
        xr = st_scr[j, :, 0:mpb]
        xi = st_scr[j, :, mpb:2 * mpb]
        for s in range(tl):
            r0 = pl.multiple_of((s + d * (tl - 1 - 2 * s)) * nb, nb)
            br = bu_scr[cur, pl.ds(r0, nb), 0:mpb]
            bi = bu_scr[cur, pl.ds(r0, nb), mpb:2 * mpb]
            xr, xi = lr * xr - li * xi + br, lr * xi + li * xr + bi
            bu_scr[cur, pl.ds(r0, nb), 0:mpb] = xr
            bu_scr[cur, pl.ds(r0, nb), mpb:2 * mpb] = xi
        st_scr[j, :, 0:mpb] = xr
        st_scr[j, :, mpb:2 * mpb] = xi

    if emit_y:
        readout(nj - 1)
        for cp in out_copies(g, slot):
            cp.start()

        @pl.when(g == ng - 1)
        def _():
            for cp in out_copies(g - 1, 1 - slot):
                cp.wait()
            for cp in out_copies(g, slot):
                cp.wait()

    @pl.when(t == nt - 1)
    def _():
        fin_ref[0] = st_scr[...]


def _s5_scan(u_arr, bw, cw, lamr, lami, init, tl, emit_y):
    b, l = u_arr.shape[:2]
    assert b == SUBLANES and l % tl == 0
    nj = bw.shape[1]
    c = nj * LANES
    nt = l // tl
    m2 = 2 * MODES_PER_BLOCK
    wmap = lambda g: (g // nt, 0, 0, 0)
    in_specs = [pl.BlockSpec(memory_space=pl.ANY), pl.BlockSpec((1, nj, LANES, m2), wmap)]
    args = [u_arr, bw]
    if emit_y:
        in_specs.append(pl.BlockSpec((1, nj, m2, LANES), wmap))
        args.append(cw)
    in_specs += [pl.BlockSpec((1, nj, 1, MODES_PER_BLOCK), wmap),
                 pl.BlockSpec((1, nj, 1, MODES_PER_BLOCK), wmap),
                 pl.BlockSpec((1, nj, SUBLANES, m2), wmap)]
    args += [lamr, lami, init]
    fin_spec = pl.BlockSpec((1, nj, SUBLANES, m2), wmap)
    fin_shape = jax.ShapeDtypeStruct((2, nj, SUBLANES, m2), F32)
    scratch = [pltpu.VMEM((2, tl, SUBLANES, c), F32),
               pltpu.VMEM((3, tl * SUBLANES, m2), F32),
               pltpu.VMEM((nj, SUBLANES, m2), F32),
               pltpu.SemaphoreType.DMA((2,))]
    if emit_y:
        out_specs = [pl.BlockSpec(memory_space=pl.ANY), fin_spec]
        out_shape = [jax.ShapeDtypeStruct((2, b, l, c), F32), fin_shape]
        scratch += [pltpu.VMEM((2, tl, SUBLANES, c), F32), pltpu.SemaphoreType.DMA((2,))]
    else:
        out_specs = [fin_spec]
        out_shape = [fin_shape]
    return pl.pallas_call(
        functools.partial(_s5_body, tl=tl, nt=nt, nj=nj, emit_y=emit_y),
        grid=(2 * nt,),
        in_specs=in_specs, out_specs=out_specs, out_shape=out_shape,
        scratch_shapes=scratch,
        compiler_params=_cparams(("arbitrary",), 48),
        name="s5scan_lat" if emit_y else "s5scan_ctx",
    )(*args)


def _hyfilt_body(z_ref, w1_ref, b1_ref, w2_ref, b2_ref, w3_ref, b3_ref, fr_ref,
                 w4f_ref, w4b_ref, df_ref, db_ref, ka_ref, kd_ref, h_scr, *, length):
    z = z_ref[...]

    @pl.when(pl.program_id(0) == 0)
    def _():
        fr = fr_ref[...]
        h = jnp.sin(fr * (_dot_hi(z, w1_ref[...]) + b1_ref[...]))
        h = jnp.sin(fr * (_dot_hi(h, w2_ref[...]) + b2_ref[...]))
        h_scr[...] = jnp.sin(fr * (_dot_hi(h, w3_ref[...]) + b3_ref[...]))

    h = h_scr[...]
    t = z[:, 0:1]
    hf = _dot_hi(h, w4f_ref[...]) * jnp.exp(-t * jnp.abs(df_ref[...]))
    hb = _dot_hi(h, w4b_ref[...]) * jnp.exp(-t * jnp.abs(db_ref[...]))
    row = lax.broadcasted_iota(I32, hf.shape, 0)
    hb = jnp.where(row < length - 1, hb, 0.0)
    norm = jnp.sum(jnp.abs(hf), axis=0, keepdims=True) + jnp.sum(jnp.abs(hb), axis=0, keepdims=True)
    kf = hf / norm
    kb = jnp.where(row >= 1, pltpu.roll(hb, 1, axis=0), 0.0) / norm
    ka_ref[...] = kf + kb
    kd_ref[...] = kf - kb


def _hyena_filter(z, w1, b1, w2, b2, w3, b3, fr, w4, deltas, ct=256):
    length, ze = z.shape
    order = w2.shape[0]
    c = w4.shape[1] // 2
    nct = c // ct
    full = lambda shape: pl.BlockSpec(shape, lambda i: (0, 0))
    shp = jax.ShapeDtypeStruct((length, c), F32)
    return pl.pallas_call(
        functools.partial(_hyfilt_body, length=length),
        grid=(nct,),
        in_specs=[full((length, ze)), full((ze, order)), full((1, order)),
                  full((order, order)), full((1, order)), full((order, order)), full((1, order)),
                  full((1, order)),
                  pl.BlockSpec((order, ct), lambda i: (0, i)),
                  pl.BlockSpec((order, ct), lambda i: (0, nct + i)),
                  pl.BlockSpec((1, ct), lambda i: (0, i)),
                  pl.BlockSpec((1, ct), lambda i: (0, nct + i))],
        out_specs=[pl.BlockSpec((length, ct), lambda i: (0, i))] * 2,
        out_shape=[shp, shp],
        scratch_shapes=[pltpu.VMEM((length, order), F32)],
        compiler_params=_cparams(("arbitrary",), 48),
        name="hyena_filter",
    )(z, w1, b1, w2, b2, w3, b3, fr, w4, w4, deltas, deltas)


def _dfttab_body(ca_ref, sa_ref, cb_ref, sb_ref, c_ref, s_ref):
    ca = ca_ref[0]
    sa = sa_ref[0]
    cb = cb_ref[...]
    sb = sb_ref[...]
    c_ref[...] = (ca * cb - sa * sb).astype(c_ref.dtype)
    s_ref[...] = (sa * cb + ca * sb).astype(s_ref.dtype)


def _dft_tables(length, r=128):
    n = 2 * length
    s = jnp.arange(length, dtype=I32)[None, :]
    ang = lambda k: (2.0 * math.pi / n) * ((k * s) % n).astype(F32)
    aa = ang(jnp.arange(length // r, dtype=I32)[:, None] * r)
    ab = ang(jnp.arange(r, dtype=I32)[:, None])
    rows3 = lambda a: a.reshape(length // r, 1, length)
    shp = jax.ShapeDtypeStruct((length, length), BF16)
    return pl.pallas_call(
        _dfttab_body,
        grid=(length // r,),
        in_specs=[pl.BlockSpec((None, 1, length), lambda i: (i, 0, 0)),
                  pl.BlockSpec((None, 1, length), lambda i: (i, 0, 0)),
                  pl.BlockSpec((r, length), lambda i: (0, 0)),
                  pl.BlockSpec((r, length), lambda i: (0, 0))],
        out_specs=[pl.BlockSpec((r, length), lambda i: (i, 0))] * 2,
        out_shape=[shp, shp],
        compiler_params=_cparams(("arbitrary",), 32),
        name="dft_tables",
    )(rows3(jnp.cos(aa)), rows3(jnp.sin(aa)), jnp.cos(ab), jnp.sin(ab))


def _hyspec_body(c_ref, s_ref, ka_ref, kd_ref, kk_ref, kn_ref, *, length, fq):
    q = pl.program_id(1)
    nq = length // fq
    n = 2.0 * length

    @pl.when(q < nq)
    def _():
        grow = q * fq + lax.broadcasted_iota(I32, (fq, 1), 0)
        scale = jnp.where(grow == 0, 1.0 / n, 2.0 / n)
        kk_ref[...] = _dot(c_ref[...], ka_ref[...].astype(BF16)) * scale

    @pl.when(q >= nq)
    def _():
        kk_ref[...] = _dot(s_ref[...], kd_ref[...].astype(BF16)) * (2.0 / n)

    @pl.when(q == 0)
    def _():
        a = ka_ref[...]
        row = lax.broadcasted_iota(I32, a.shape, 0)
        sgn = jnp.where(row % 2 == 0, 1.0, -1.0)
        kn = jnp.sum(a * sgn, axis=0, keepdims=True) * (1.0 / n)
        kn_ref[...] = jnp.broadcast_to(kn, kn_ref.shape)


def _hyena_spectrum(ctab, stab, ka, kd, ct=256, fq=512):
    length, c = ka.shape
    nq = length // fq
    return pl.pallas_call(
        functools.partial(_hyspec_body, length=length, fq=fq),
        grid=(c // ct, 2 * nq),
        in_specs=[pl.BlockSpec((fq, length), lambda i, q: (jnp.minimum(q, nq - 1), 0)),
                  pl.BlockSpec((fq, length), lambda i, q: (jnp.maximum(q - nq, 0), 0)),
                  pl.BlockSpec((length, ct), lambda i, q: (0, i)),
                  pl.BlockSpec((length, ct), lambda i, q: (0, i))],
        out_specs=[pl.BlockSpec((fq, ct), lambda i, q: (q, i)),
                   pl.BlockSpec((SUBLANES, ct), lambda i, q: (0, i))],
        out_shape=[jax.ShapeDtypeStruct((2 * length, c), F32), jax.ShapeDtypeStruct((SUBLANES, c), F32)],
        compiler_params=_cparams(("arbitrary", "arbitrary"), 48),
        name="hyena_spectrum",
    )(ctab, stab, ka, kd)


def _hypre_body(pv_ref, p1_ref, p0_ref, wv_ref, w1_ref, w0_ref, bv_ref, b1_ref, b0_ref, v_ref, x0_ref):
    tm = pv_ref.shape[1]
    row = lax.broadcasted_iota(I32, (tm, 1), 0) % GRID_W

    def sconv(p_ref, w_ref, b_ref):
        a = p_ref[0]
        prev = jnp.where(row == 0, 0.0, pltpu.roll(a, 1, axis=0))
        nxt = jnp.where(row == GRID_W - 1, 0.0, pltpu.roll(a, tm - 1, axis=0))
        return prev * w_ref[0:1, :] + a * w_ref[1:2, :] + nxt * w_ref[2:3, :] + b_ref[...]

    v_ref[0] = (sconv(pv_ref, wv_ref, bv_ref) * sconv(p1_ref, w1_ref, b1_ref)).astype(v_ref.dtype)
    x0_ref[0] = sconv(p0_ref, w0_ref, b0_ref).astype(x0_ref.dtype)


def _hyena_pre(p, conv_w, conv_b, c, first_block, tm=512):
    b, l, _ = p.shape
    pspec = lambda k: pl.BlockSpec((1, tm, c), lambda bi, li: (bi, li, first_block + k))
    wspec = lambda k: pl.BlockSpec((3, c), lambda bi, li: (0, k))
    bspec = lambda k: pl.BlockSpec((1, c), lambda bi, li: (0, k))
    shp = jax.ShapeDtypeStruct((b, l, c), BF16)
    return pl.pallas_call(
        _hypre_body,
        grid=(b, l // tm),
        in_specs=[pspec(0), pspec(1), pspec(2), wspec(0), wspec(1), wspec(2), bspec(0), bspec(1), bspec(2)],
        out_specs=[pl.BlockSpec((1, tm, c), lambda bi, li: (bi, li, 0))] * 2,
        out_shape=[shp, shp],
        compiler_params=_cparams(("arbitrary", "arbitrary"), 48),
        name="hyena_pre",
    )(p, p, p, conv_w, conv_w, conv_w, conv_b, conv_b, conv_b)


def _hyconv_body(v_ref, x0_ref, kk_ref, kn_ref, d_ref, c_hbm, s_hbm, o_ref, c_scr, s_scr, sem, *, length, fq):
    first = jnp.logical_and(pl.program_id(0) == 0, pl.program_id(1) == 0)

    @pl.when(first)
    def _():
        cc = pltpu.make_async_copy(c_hbm, c_scr, sem.at[0])
        cs = pltpu.make_async_copy(s_hbm, s_scr, sem.at[1])
        cc.start()
        cs.start()
        cc.wait()
        cs.wait()

    vb = v_ref[0]
    v = vb.astype(F32)
    row = lax.broadcasted_iota(I32, (length, 1), 0)
    sgn = jnp.where(row % 2 == 0, 1.0, -1.0)
    vn = jnp.sum(v * sgn, axis=0, keepdims=True)
    acc = sgn * (vn * kn_ref[0:1, :])
    for q in range(length // fq):
        lo, hi = q * fq, (q + 1) * fq
        vr = _dot(c_scr[lo:hi, :], vb)
        vs = _dot(s_scr[lo:hi, :], vb)
        kr = kk_ref[lo:hi, :]
        ks = kk_ref[length + lo:length + hi, :]
        pr = vr * kr - vs * ks
        ps = vr * ks + vs * kr
        acc = acc + _dot(c_scr[:, lo:hi], pr.astype(BF16)) + _dot(s_scr[:, lo:hi], ps.astype(BF16))
    o_ref[0] = ((acc + v * d_ref[...]) * x0_ref[0].astype(F32)).astype(o_ref.dtype)


def _hyena_conv(v, x0, kk, kn, d_skip, ctab, stab, ct=256, fq=512):
    b, l, c = v.shape
    return pl.pallas_call(
        functools.partial(_hyconv_body, length=l, fq=fq),
        grid=(c // ct, b),
        in_specs=[pl.BlockSpec((1, l, ct), lambda ci, bi: (bi, 0, ci)),
                  pl.BlockSpec((1, l, ct), lambda ci, bi: (bi, 0, ci)),
                  pl.BlockSpec((2 * l, ct), lambda ci, bi: (0, ci)),
                  pl.BlockSpec((SUBLANES, ct), lambda ci, bi: (0, ci)),
                  pl.BlockSpec((1, ct), lambda ci, bi: (0, ci)),
                  pl.BlockSpec(memory_space=pl.ANY),
                  pl.BlockSpec(memory_space=pl.ANY)],
        out_specs=pl.BlockSpec((1, l, ct), lambda ci, bi: (bi, 0, ci)),
        out_shape=jax.ShapeDtypeStruct((b, l, c), BF16),
        scratch_shapes=[pltpu.VMEM((l, l), BF16), pltpu.VMEM((l, l), BF16),
                        pltpu.SemaphoreType.DMA((2,))],
        compiler_params=_cparams(("arbitrary", "arbitrary"), 48),
        name="hyena_conv",
    )(v, x0, kk, kn, d_skip, ctab, stab)


def _split_bf16(a):
    hi = a.astype(BF16)
    return hi, (a - hi.astype(F32)).astype(BF16)


def _outproj_body(y_ref, u_ref, yh_ref, x_ref, d_ref, wg_ref, bg_ref, wo_ref, ga_ref, g2_ref,
                  sh_ref, sc_ref, wrh_ref, wrl_ref, br_ref, x1_ref, hp_ref, te_ref, gt_ref, cnt_ref, *, sub):
    tm, c = u_ref.shape[1], u_ref.shape[2]

    @pl.when(jnp.logical_and(pl.program_id(0) == 0, pl.program_id(1) == 0))
    def _():
        cnt_ref[...] = jnp.zeros_like(cnt_ref)

    lane = lax.broadcasted_iota(I32, (sub, LANES), 1)
    total = jnp.zeros((1, LANES), F32)
    for r0 in range(0, tm, sub):
        rs = slice(r0, r0 + sub)
        ys = y_ref[0, 0, rs, :] + y_ref[1, 0, rs, :] + u_ref[0, rs, :] * d_ref[...]
        ge = 0.5 * ys * (1.0 + lax.erf(ys * (1.0 / math.sqrt(2.0))))
        gl = ge * jax.nn.sigmoid(_dot(ge.astype(BF16), wg_ref[...]) + bg_ref[...])
        mix = _dot(gl.astype(BF16), wo_ref[0:c, :]) + _dot(yh_ref[0, rs, :], wo_ref[c:2 * c, :])
        x1 = x_ref[0, rs, :] + ga_ref[0] * mix
        x1_ref[0, rs, :] = x1
        h2 = _norm_mod(x1, g2_ref[...], sh_ref[0], sc_ref[0])
        hp_ref[0, rs, :] = h2

        hh, hl = _split_bf16(h2)
        logits = _dot(hh, wrh_ref[...]) + (_dot(hh, wrl_ref[...]) + _dot(hl, wrh_ref[...])) + br_ref[...]
        work = logits
        vals, idxs = [], []
        for _ in range(TOP_K):
            m = jnp.max(work, axis=-1, keepdims=True)
            ix = jnp.min(jnp.where(work == m, lane, LANES), axis=-1, keepdims=True)
            vals.append(m)
            idxs.append(ix)
            work = jnp.where(lane == ix, NEG_BIG, work)
        ex = [jnp.exp(v - vals[0]) for v in vals]
        den = ex[0]
        for e in ex[1:]:
            den = den + e
        te = jnp.zeros((sub, LANES), I32)
        gt = jnp.zeros((sub, LANES), F32)
        mh = jnp.zeros((sub, LANES), F32)
        for k in range(TOP_K):
            te = jnp.where(lane == k, idxs[k], te)
            gt = jnp.where(lane == k, ex[k] / den, gt)
            mh = mh + (lane == idxs[k]).astype(F32)
        te_ref[0, rs, :] = te
        gt_ref[0, rs, :] = gt
        total = total + jnp.sum(mh, axis=0, keepdims=True)

    cnt_ref[...] += jnp.broadcast_to(total, cnt_ref.shape)


def _outproj(y, p, yh, x, d_skip, w_glu, b_glu, w_out, ga1, g2, sh2, sc2, w_rh, w_rl, b_r, tm=256, sub=128):
    b, l, dm = x.shape
    c = yh.shape[2]
    tok = lambda width: pl.BlockSpec((1, tm, width), lambda bi, li: (bi, li, 0))
    full = lambda shape: pl.BlockSpec(shape, lambda bi, li: (0,) * len(shape))
    perb = pl.BlockSpec((1, 1, dm), lambda bi, li: (bi, 0, 0))
    return pl.pallas_call(
        functools.partial(_outproj_body, sub=sub),
        grid=(b, l // tm),
        in_specs=[pl.BlockSpec((2, 1, tm, c), lambda bi, li: (0, bi, li, 0)),
                  tok(c), tok(c), tok(dm),
                  full((1, c)), full((c, c)), full((1, c)), full((2 * c, dm)),
                  perb, full((1, dm)), perb, perb,
                  full((dm, LANES)), full((dm, LANES)), full((1, LANES))],
        out_specs=[tok(dm), tok(dm), tok(LANES), tok(LANES), full((SUBLANES, LANES))],
        out_shape=[jax.ShapeDtypeStruct((b, l, dm), F32),
                   jax.ShapeDtypeStruct((b, l, dm), F32),
                   jax.ShapeDtypeStruct((b, l, LANES), I32),
                   jax.ShapeDtypeStruct((b, l, LANES), F32),
                   jax.ShapeDtypeStruct((SUBLANES, LANES), F32)],
        compiler_params=_cparams(("arbitrary", "arbitrary"), 56),
        name="outproj_router",
    )(y, p, yh, x, d_skip, w_glu, b_glu, w_out, ga1, g2, sh2, sc2, w_rh, w_rl, b_r)


def _slots_body(te_ref, seg_ref, pos_ref, carry):
    @pl.when(pl.program_id(0) == 0)
    def _():
        carry[...] = jnp.zeros_like(carry)

    te = te_ref[...]
    tm = te.shape[0]
    lane = lax.broadcasted_iota(I32, (tm, LANES), 1)
    sel = [lane == te[:, k:k + 1] for k in range(TOP_K)]
    mh = jnp.zeros((tm, LANES), F32)
    for s in sel:
        mh = mh + s.astype(F32)
    r = lax.broadcasted_iota(I32, (tm, tm), 0)
    cc = lax.broadcasted_iota(I32, (tm, tm), 1)
    strict_lower = (r > cc).astype(BF16)
    slot = _dot(strict_lower, mh.astype(BF16)) + carry[0:1, :] + seg_ref[0:1, :]
    pos = jnp.zeros((tm, LANES), F32)
    for k in range(TOP_K):
        pk = jnp.sum(jnp.where(sel[k], slot, 0.0), axis=-1, keepdims=True)
        pos = jnp.where(lane == k, pk, pos)
    pos_ref[...] = pos.astype(I32)
    carry[...] += jnp.broadcast_to(jnp.sum(mh, axis=0, keepdims=True), carry.shape)


def _slots(te, seg_start, tm=512):
    t = te.shape[0]
    return pl.pallas_call(
        _slots_body,
        grid=(t // tm,),
        in_specs=[pl.BlockSpec((tm, LANES), lambda i: (i, 0)),
                  pl.BlockSpec((SUBLANES, LANES), lambda i: (0, 0))],
        out_specs=pl.BlockSpec((tm, LANES), lambda i: (i, 0)),
        out_shape=jax.ShapeDtypeStruct((t, LANES), I32),
        scratch_shapes=[pltpu.VMEM((SUBLANES, LANES), F32)],
        compiler_params=_cparams(("arbitrary",), 32),
        name="moe_slots",
    )(te, seg_start)


PAD_CHUNKS = tuple(s for s in (ROW_BLOCK >> k for k in range(1, ROW_BLOCK.bit_length())) if s >= SUBLANES)


def _row_copy(src, src_row, dst, dst_row, sem):
    return pltpu.make_async_copy(src.at[pl.ds(src_row, 1)], dst.at[pl.ds(dst_row, 1)], sem)


def _dispatch_body(pos_ref, ps_ref, pn_ref, h_ref, xs_out, zbuf, sem, zsem, *, tm, n_exp, n_blocks):
    @pl.when(pl.program_id(0) == 0)
    def _():
        zbuf[...] = jnp.zeros_like(zbuf)

        def copies(e):
            start = ps_ref[e]
            head = (-start) & (SUBLANES - 1)
            out = [(r < head, _row_copy(zbuf, 0, xs_out, start + r, zsem)) for r in range(SUBLANES - 1)]
            body = pn_ref[e] - head
            for size in PAD_CHUNKS:
                done = body & (-2 * size)
                dst = xs_out.at[pl.ds(pl.multiple_of(start + head + done, SUBLANES), size)]
                out.append(((body & size) != 0, pltpu.make_async_copy(zbuf.at[pl.ds(0, size)], dst, zsem)))
            return out

        def fill(e, carry):
            for on, cp in copies(e):
                pl.when(on)(cp.start)
            return carry

        def drain(e, carry):
            for on, cp in copies(e):
                pl.when(on)(cp.wait)
            return carry

        lax.fori_loop(0, n_exp, fill, 0)
        lax.fori_loop(0, n_exp, drain, 0)

        def tail(blk):
            rows = [pl.multiple_of(blk * ROW_BLOCK + part * PAD_CHUNKS[0], SUBLANES)
                    for part in range(ROW_BLOCK // PAD_CHUNKS[0])]
            return [pltpu.make_async_copy(zbuf, xs_out.at[pl.ds(r, PAD_CHUNKS[0])], zsem) for r in rows]

        def fill_tail(blk, carry):
            for cp in tail(blk):
                cp.start()
            return carry

        def drain_tail(blk, carry):
            for cp in tail(blk):
                cp.wait()
            return carry

        first_unused = (ps_ref[n_exp - 1] + pn_ref[n_exp - 1]) // ROW_BLOCK
        lax.fori_loop(first_unused, n_blocks, fill_tail, 0)
        lax.fori_loop(first_unused, n_blocks, drain_tail, 0)

    base = pl.program_id(0) * (tm * TOP_K)

    def issue(r, carry):
        for k in range(TOP_K):
            _row_copy(h_ref, r, xs_out, pos_ref[base + r * TOP_K + k], sem).start()
        return carry

    lax.fori_loop(0, tm, issue, 0, unroll=8)

    def drain_rows(r, carry):
        for k in range(TOP_K):
            _row_copy(h_ref, 0, xs_out, 0, sem).wait()
        return carry

    lax.fori_loop(0, tm, drain_rows, 0, unroll=8)


def _dispatch(pos_flat, pad_start, pad_len, hp, n_rows, tm=256):
    t, w = hp.shape
    return pl.pallas_call(
        functools.partial(_dispatch_body, tm=tm, n_exp=pad_start.shape[0], n_blocks=n_rows // ROW_BLOCK),
        grid_spec=pltpu.PrefetchScalarGridSpec(
            num_scalar_prefetch=3,
            grid=(t // tm,),
            in_specs=[pl.BlockSpec((tm, w), lambda i, pos, ps, pn: (i, 0))],
            out_specs=pl.BlockSpec(memory_space=pl.ANY),
            scratch_shapes=[pltpu.VMEM((PAD_CHUNKS[0], w), hp.dtype),
                            pltpu.SemaphoreType.DMA(()), pltpu.SemaphoreType.DMA(())]),
        out_shape=jax.ShapeDtypeStruct((n_rows, w), hp.dtype),
        compiler_params=_cparams(("arbitrary",), 32),
        name="moe_dispatch",
    )(pos_flat, pad_start, pad_len, hp)


def _stream_expert_blocks(first, count, total, n_blocks, src_hbm, dst_hbm, col0, ibuf, obuf, isem, osem, compute):
    e = pl.program_id(1)
    rb = ROW_BLOCK
    width = obuf.shape[2]

    def rows(gs):
        return pl.ds(pl.multiple_of(gs * rb, rb), rb)

    def in_copy(gs, slot):
        return pltpu.make_async_copy(src_hbm.at[rows(gs)], ibuf.at[slot], isem.at[slot])

    def out_copy(gs, slot):
        return pltpu.make_async_copy(obuf.at[slot], dst_hbm.at[rows(gs), pl.ds(col0, width)], osem.at[slot])

    @pl.when(e == 0)
    def _():
        in_copy(0, 0).start()

    def block(s, carry):
        gs = first + s
        slot = gs % 2

        @pl.when(gs + 1 < total)
        def _():
            in_copy(gs + 1, 1 - slot).start()

        in_copy(gs, slot).wait()

        @pl.when(gs >= 2)
        def _():
            out_copy(gs - 2, slot).wait()

        compute(ibuf.at[slot], obuf.at[slot])
        out_copy(gs, slot).start()
        return carry

    lax.fori_loop(0, count, block, 0)

    @pl.when(e == pl.num_programs(1) - 1)
    def _():
        @pl.when(total >= 2)
        def _():
            out_copy(total - 2, total % 2).wait()

        out_copy(total - 1, (total - 1) % 2).wait()
        obuf[0] = jnp.zeros(obuf.shape[1:], obuf.dtype)

        def fill(gs, carry):
            out_copy(gs, 0).start()
            return carry

        def drain(gs, carry):
            out_copy(gs, 0).wait()
            return carry

        lax.fori_loop(total, n_blocks, fill, 0)
        lax.fori_loop(total, n_blocks, drain, 0)


def _gateup_body(fs_ref, ns_ref, tot_ref, xs_hbm, wg_ref, wu_ref, bg_ref, bu_ref, h_hbm,
                 wg_s, wu_s, xbuf, obuf, xsem, osem, *, n_blocks):
    e = pl.program_id(1)
    tn = obuf.shape[2]
    half = tn // 2

    @pl.when(ns_ref[e] > 0)
    def _():
        wg_s[...] = wg_ref[...].astype(BF16)
        wu_s[...] = wu_ref[...].astype(BF16)

    def compute(x_ref, o_ref):
        x = x_ref[...].astype(BF16)
        for c0 in (0, half):
            cs = slice(c0, c0 + half)
            g = _dot(x, wg_s[:, cs]) + bg_ref[:, cs]
            u = _dot(x, wu_s[:, cs]) + bu_ref[:, cs]
            g = jnp.minimum(g, SWIGLU_LIMIT)
            u = jnp.clip(u, -SWIGLU_LIMIT, SWIGLU_LIMIT)
            o_ref[:, cs] = ((u + 1.0) * (g * jax.nn.sigmoid(SWIGLU_ALPHA * g))).astype(o_ref.dtype)

    col0 = pl.multiple_of(pl.program_id(0) * tn, tn)
    _stream_expert_blocks(fs_ref[e], ns_ref[e], tot_ref[0], n_blocks, xs_hbm, h_hbm, col0,
                          xbuf, obuf, xsem, osem, compute)


def _expert_gateup(first_blk, n_blk, total, xs, w_gate_up, b_gate_up, tn=1024):
    n_exp, dm, ff2 = w_gate_up.shape
    ff = ff2 // 2
    nr = xs.shape[0]
    nn = ff // tn
    return pl.pallas_call(
        functools.partial(_gateup_body, n_blocks=nr // ROW_BLOCK),
        grid_spec=pltpu.PrefetchScalarGridSpec(
            num_scalar_prefetch=3,
            grid=(nn, n_exp),
            in_specs=[pl.BlockSpec(memory_space=pl.ANY),
                      pl.BlockSpec((None, dm, tn), lambda j, e, fs, ns, tot: (e, 0, j)),
                      pl.BlockSpec((None, dm, tn), lambda j, e, fs, ns, tot: (e, 0, nn + j)),
                      pl.BlockSpec((None, 1, tn), lambda j, e, fs, ns, tot: (e, 0, j)),
                      pl.BlockSpec((None, 1, tn), lambda j, e, fs, ns, tot: (e, 0, nn + j))],
            out_specs=pl.BlockSpec(memory_space=pl.ANY),
            scratch_shapes=[pltpu.VMEM((dm, tn), BF16), pltpu.VMEM((dm, tn), BF16),
                            pltpu.VMEM((2, ROW_BLOCK, dm), xs.dtype), pltpu.VMEM((2, ROW_BLOCK, tn), BF16),
                            pltpu.SemaphoreType.DMA((2,)), pltpu.SemaphoreType.DMA((2,))]),
        out_shape=jax.ShapeDtypeStruct((nr, ff), BF16),
        compiler_params=_cparams(("arbitrary", "arbitrary"), 56),
        name="moe_gateup",
    )(first_blk, n_blk, total, xs, w_gate_up, w_gate_up, b_gate_up, b_gate_up)


def _down_body(fs_ref, ns_ref, tot_ref, h_hbm, w_ref, b_ref, y_hbm, w_s, hbuf, obuf, hsem, osem, *, n_blocks):
    e = pl.program_id(1)
    tn = obuf.shape[2]

    @pl.when(ns_ref[e] > 0)
    def _():
        w_s[...] = w_ref[...].astype(BF16)

    def compute(h_ref, o_ref):
        o_ref[...] = _dot(h_ref[...], w_s[...]) + b_ref[...]

    col0 = pl.multiple_of(pl.program_id(0) * tn, tn)
    _stream_expert_blocks(fs_ref[e], ns_ref[e], tot_ref[0], n_blocks, h_hbm, y_hbm, col0,
                          hbuf, obuf, hsem, osem, compute)


def _expert_down(first_blk, n_blk, total, h, w_down, b_down, tn=1024):
    n_exp, ff, dm = w_down.shape
    nr = h.shape[0]
    return pl.pallas_call(
        functools.partial(_down_body, n_blocks=nr // ROW_BLOCK),
        grid_spec=pltpu.PrefetchScalarGridSpec(
            num_scalar_prefetch=3,
            grid=(dm // tn, n_exp),
            in_specs=[pl.BlockSpec(memory_space=pl.ANY),
                      pl.BlockSpec((None, ff, tn), lambda j, e, fs, ns, tot: (e, 0, j)),
                      pl.BlockSpec((None, 1, tn), lambda j, e, fs, ns, tot: (e, 0, j))],
            out_specs=pl.BlockSpec(memory_space=pl.ANY),
            scratch_shapes=[pltpu.VMEM((ff, tn), BF16),
                            pltpu.VMEM((2, ROW_BLOCK, ff), h.dtype), pltpu.VMEM((2, ROW_BLOCK, tn), F32),
                            pltpu.SemaphoreType.DMA((2,)), pltpu.SemaphoreType.DMA((2,))]),
        out_shape=jax.ShapeDtypeStruct((nr, dm), F32),
        compiler_params=_cparams(("arbitrary", "arbitrary"), 48),
        name="moe_down",
    )(first_blk, n_blk, total, h, w_down, b_down)


def _combine_body(pos_ref, x1_ref, gt_ref, ga_ref, gf_ref, y_hbm, o_ref, ybuf, sem, *, tm):
    i = pl.program_id(0)
    n = pl.num_programs(0)

    def gather(tile, slot):
        def issue(r, carry):
            for k in range(TOP_K):
                p = pos_ref[(tile * tm + r) * TOP_K + k]
                _row_copy(y_hbm, p, ybuf.at[slot, k], r, sem.at[slot]).start()
            return carry
        lax.fori_loop(0, tm, issue, 0, unroll=8)

    @pl.when(i == 0)
    def _():
        gather(0, 0)

    @pl.when(i + 1 < n)
    def _():
        gather(i + 1, (i + 1) % 2)

    slot = i % 2

    def drain(r, carry):
        for k in range(TOP_K):
            _row_copy(y_hbm, 0, ybuf.at[slot, k], 0, sem.at[slot]).wait()
        return carry

    lax.fori_loop(0, tm, drain, 0, unroll=8)

    gt = gt_ref[0]
    acc = gt[:, 0:1] * ybuf[slot, 0]
    for k in range(1, TOP_K):
        acc = acc + gt[:, k:k + 1] * ybuf[slot, k]
    x2 = x1_ref[0] + ga_ref[0] * acc
    ms = jnp.mean(x2 * x2, axis=-1, keepdims=True)
    o_ref[0] = x2 * lax.rsqrt(ms + EPS) * gf_ref[...]


def _combine(pos_flat, x1, gate, ga2, g_final, y, tm=128):
    b, l, dm = x1.shape
    per = l // tm
    tok = lambda width: pl.BlockSpec((1, tm, width), lambda i, pos: (i // per, i % per, 0))
    return pl.pallas_call(
        functools.partial(_combine_body, tm=tm),
        grid_spec=pltpu.PrefetchScalarGridSpec(
            num_scalar_prefetch=1,
            grid=(b * per,),
            in_specs=[tok(dm), tok(LANES),
                      pl.BlockSpec((1, 1, dm), lambda i, pos: (i // per, 0, 0)),
                      pl.BlockSpec((1, dm), lambda i, pos: (0, 0)),
                      pl.BlockSpec(memory_space=pl.ANY)],
            out_specs=tok(dm),
            scratch_shapes=[pltpu.VMEM((2, TOP_K, tm, dm), F32), pltpu.SemaphoreType.DMA((2,))]),
        out_shape=jax.ShapeDtypeStruct((b, l, dm), F32),
        compiler_params=_cparams(("arbitrary",), 48),
        name="moe_combine",
    )(pos_flat, x1, gate, ga2, g_final, y)


def _filter_features(length):
    t = jnp.linspace(0.0, 1.0, length, dtype=F32)[:, None]
    bands = jnp.linspace(1e-4, HY_BANDS - 1, HY_BANDS, dtype=F32)
    ang = (2 * math.pi / length) * jnp.arange(length, dtype=F32)[:, None] * bands
    z = jnp.concatenate([t, jnp.cos(ang), -jnp.sin(ang)], axis=-1)
    return jnp.pad(z, ((0, 0), (0, LANES - z.shape[1])))


def _segments(counts):
    padded = (counts + ROW_BLOCK - 1) // ROW_BLOCK * ROW_BLOCK
    seg_end = jnp.cumsum(padded)
    seg_start = seg_end - padded
    total = (seg_end[-1:] // ROW_BLOCK).astype(I32)
    return (seg_start, (seg_start // ROW_BLOCK).astype(I32), (padded // ROW_BLOCK).astype(I32), total,
            seg_start + counts, padded - counts)


def kernel(x, c, ctx, c_ctx, w_ada, b_ada, g_norm1, g_norm2, w_in, s5_lam_re, s5_lam_im, s5_log_step,
           s5_b_re, s5_b_im, s5_c_re, s5_c_im, s5_d, s5_w_glu, s5_b_glu, hy_conv_w, hy_conv_b,
           hy_w1, hy_b1, hy_w2, hy_b2, hy_w3, hy_b3, hy_w4, hy_freq, hy_deltas, hy_d, w_out,
           w_router, b_router, w_gate_up, b_gate_up, w_down, b_down, g_final):
    assert w_ada.shape[0] == 1, "single-layer block"
    b, l, dm = x.shape
    lc = ctx.shape[1]
    s5w = s5_d.shape[1]
    hyw = hy_d.shape[1]
    n_exp = w_router.shape[2]
    row = lambda a: a.reshape(1, -1)

    cc = jnp.concatenate([c, c_ctx[None], jnp.zeros((2 * SUBLANES - b - 1, dm), F32)], axis=0)
    mod = _adaln(cc, w_ada[0], row(b_ada[0]))
    sh1, sc1, ga1, sh2, sc2, ga2 = [mod[:b, k * dm:(k + 1) * dm].reshape(b, 1, dm) for k in range(N_MOD)]
    csh1, csc1 = [mod[b:b + 1, k * dm:(k + 1) * dm].reshape(1, 1, dm) for k in range(2)]

    w_in_b = w_in[0].astype(BF16)
    g1 = row(g_norm1[0])
    p = _inproj(x, g1, sh1, sc1, w_in_b, w_in_b.shape[1], tm=512)
    p_ctx = _inproj(ctx, g1, csh1, csc1, w_in_b, s5w, tm=lc)

    bw, cw, lamr, lami = _s5_weights(s5_lam_re[0], s5_lam_im[0], s5_log_step[0],
                                     s5_b_re[0], s5_b_im[0], s5_c_re[0], s5_c_im[0])
    zero_state = jnp.zeros((2, bw.shape[1], SUBLANES, 2 * MODES_PER_BLOCK), F32)
    (ctx_state,) = _s5_scan(p_ctx, bw, cw, lamr, lami, zero_state, tl=64, emit_y=False)
    y_s5, _ = _s5_scan(p, bw, cw, lamr, lami, ctx_state, tl=64, emit_y=True)

    ka, kd = _hyena_filter(_filter_features(l), jnp.pad(hy_w1[0], ((0, LANES - hy_w1.shape[1]), (0, 0))),
                           row(hy_b1[0]), hy_w2[0], row(hy_b2[0]), hy_w3[0], row(hy_b3[0]),
                           row(hy_freq[0]), hy_w4[0], row(hy_deltas[0]))
    ctab, stab = _dft_tables(l)
    kk, kn = _hyena_spectrum(ctab, stab, ka, kd)
    v, x0 = _hyena_pre(p, hy_conv_w[0], row(hy_conv_b[0]), hyw, s5w // hyw)
    y_hy = _hyena_conv(v, x0, kk, kn, row(hy_d[0]), ctab, stab)

    w_rh, w_rl = _split_bf16(jnp.pad(w_router[0], ((0, 0), (0, LANES - n_exp))))
    b_r = jnp.concatenate([b_router[0], jnp.full((LANES - n_exp,), NEG_BIG, F32)]).reshape(1, LANES)
    x1, h2, te, gate, cnt = _outproj(y_s5, p, y_hy, x, row(s5_d[0]), s5_w_glu[0].astype(BF16),
                                     row(s5_b_glu[0]), w_out[0].astype(BF16), ga1, row(g_norm2[0]),
                                     sh2, sc2, w_rh, w_rl, b_r)

    t = b * l
    n_blocks = t * TOP_K // ROW_BLOCK + n_exp
    counts = cnt[0, :n_exp].astype(I32)
    seg_start, first_blk, n_blk, total, pad_start, pad_len = _segments(counts)
    seg_row = jnp.broadcast_to(jnp.pad(seg_start.astype(F32), (0, LANES - n_exp)), (SUBLANES, LANES))
    pos = _slots(te.reshape(t, LANES), seg_row)
    pos_flat = pos[:, :TOP_K].reshape(-1)
    xs = _dispatch(pos_flat, pad_start, pad_len, h2.reshape(t, dm), n_blocks * ROW_BLOCK)
    hid = _expert_gateup(first_blk, n_blk, total, xs, w_gate_up[0], b_gate_up[0].reshape(n_exp, 1, -1))
    y_e = _expert_down(first_blk, n_blk, total, hid, w_down[0], b_down[0].reshape(n_exp, 1, -1))
    return _combine(pos_flat, x1, gate, ga2, row(g_final), y_e)
```

```python
import functools
import math

import jax
import jax.numpy as jnp
from jax import lax
from jax.experimental import pallas as pl
from jax.experimental.pallas import tpu as pltpu

F32 = jnp.float32
BF16 = jnp.bfloat16
I32 = jnp.int32
U32 = jnp.uint32
HIGHEST = lax.Precision.HIGHEST

GRID_W = 64
S5_H = 16
S5_P = 64
S5_RE_MAX = -1e-4
HY_BANDS = 16
N_EXPERTS = 32
TOP_K = 4
SWIGLU_LIMIT = 7.0
SWIGLU_ALPHA = 1.702
N_MOD = 6
EPS = 1e-6

LANES = 128
SUBLANES = 8
GROUPS_PER_BLOCK = LANES // S5_H
MODES_PER_BLOCK = GROUPS_PER_BLOCK * S5_P
ROW_BLOCK = 256
IN_SLOTS = 3
NEG_BIG = -1e30


def _cparams(sem, vmem_mb):
    return pltpu.CompilerParams(dimension_semantics=sem, vmem_limit_bytes=vmem_mb << 20)


def _dot(a, b):
    return jnp.dot(a, b, preferred_element_type=F32)


def _dot_hi(a, b):
    return jnp.dot(a, b, preferred_element_type=F32, precision=HIGHEST)


def _adaln_body(c_ref, w_ref, b_ref, o_ref):
    c = c_ref[...]
    s = c * jax.nn.sigmoid(c)
    o_ref[...] = _dot_hi(s, w_ref[...]) + b_ref[...]


def _adaln(cc, w, b, tn=1024):
    rows, d = cc.shape
    n = w.shape[1]
    return pl.pallas_call(
        _adaln_body,
        grid=(n // tn,),
        in_specs=[pl.BlockSpec((rows, d), lambda j: (0, 0)),
                  pl.BlockSpec((d, tn), lambda j: (0, j)),
                  pl.BlockSpec((1, tn), lambda j: (0, j))],
        out_specs=pl.BlockSpec((rows, tn), lambda j: (0, j)),
        out_shape=jax.ShapeDtypeStruct((rows, n), F32),
        compiler_params=_cparams(("arbitrary",), 40),
        name="adaln",
    )(cc, w, b)


def _norm_mod(x, g, sh, sc):
    ms = jnp.mean(x * x, axis=-1, keepdims=True)
    return (x * lax.rsqrt(ms + EPS) * g) * (1.0 + sc) + sh


def _inproj_body(x_ref, g_ref, sh_ref, sc_ref, w_ref, o_ref):
    half = x_ref.shape[1] // 2
    for r0 in (0, half):
        rs = slice(r0, r0 + half)
        h = _norm_mod(x_ref[0, rs, :], g_ref[...], sh_ref[0], sc_ref[0])
        o_ref[0, rs, :] = _dot(h.astype(BF16), w_ref[...])


def _inproj(x, g, sh, sc, w, n_cols, tm, tn=1024):
    b, l, d = x.shape
    if sh.shape[0] > 1:
        bidx = lambda n, bi, li: (bi, 0, 0)
    else:
        bidx = lambda n, bi, li: (0, 0, 0)
    return pl.pallas_call(
        _inproj_body,
        grid=(n_cols // tn, b, l // tm),
        in_specs=[pl.BlockSpec((1, tm, d), lambda n, bi, li: (bi, li, 0)),
                  pl.BlockSpec((1, d), lambda n, bi, li: (0, 0)),
                  pl.BlockSpec((1, 1, d), bidx),
                  pl.BlockSpec((1, 1, d), bidx),
                  pl.BlockSpec((d, tn), lambda n, bi, li: (0, n))],
        out_specs=pl.BlockSpec((1, tm, tn), lambda n, bi, li: (bi, li, n)),
        out_shape=jax.ShapeDtypeStruct((b, l, n_cols), F32),
        compiler_params=_cparams(("arbitrary", "arbitrary", "arbitrary"), 40),
        name="inproj",
    )(x, g, sh, sc, w)


def _s5prep_body(lre_ref, lim_ref, ls_ref, bre_ref, bim_ref, lbr_ref, lbi_ref, bbr_ref, bbi_ref):
    lr = jnp.minimum(lre_ref[...], S5_RE_MAX)
    li = lim_ref[...]
    dt = jnp.exp(ls_ref[...])
    er = jnp.exp(lr * dt)
    lbr = er * jnp.cos(li * dt)
    lbi = er * jnp.sin(li * dt)
    nr = lbr - 1.0
    den = lr * lr + li * li
    qr = (nr * lr + lbi * li) / den
    qi = (lbi * lr - nr * li) / den
    bre = bre_ref[...]
    bim = bim_ref[...]
    lbr_ref[...] = lbr
    lbi_ref[...] = lbi
    bbr_ref[...] = qr * bre - qi * bim
    bbi_ref[...] = qr * bim + qi * bre


def _s5_weights(lam_re, lam_im, log_step, b_re, b_im, c_re, c_im):
    nd, g, p = lam_re.shape
    h = S5_H
    nj = g // GROUPS_PER_BLOCK
    rep = lambda a: jnp.repeat(a.reshape(nd * g, 1, p), h, axis=1).reshape(nd * g * h, p)
    tr = lambda a: jnp.transpose(a, (0, 1, 3, 2)).reshape(nd * g * h, p)
    shp = jax.ShapeDtypeStruct((nd * g * h, p), F32)
    lbr, lbi, bbr, bbi = pl.pallas_call(
        _s5prep_body, out_shape=(shp, shp, shp, shp), name="s5prep",
    )(rep(lam_re), rep(lam_im), rep(log_step), tr(b_re), tr(b_im))
    eye = jnp.eye(GROUPS_PER_BLOCK, dtype=F32)
    bb = jnp.stack([bbr, bbi]).reshape(2, nd, nj, GROUPS_PER_BLOCK, h, p)
    bw = jnp.einsum('rdjahp,ab->djahrbp', bb, eye).reshape(nd, nj, LANES, 2 * MODES_PER_BLOCK)
    cc = jnp.stack([c_re, -c_im]).reshape(2, nd, nj, GROUPS_PER_BLOCK, h, p)
    cw = jnp.einsum('rdjahp,ab->djrapbh', cc, eye).reshape(nd, nj, 2 * MODES_PER_BLOCK, LANES)
    pick = lambda a: a.reshape(nd, g, h, p)[:, :, 0, :].reshape(nd, nj, 1, MODES_PER_BLOCK)
    return bw.astype(BF16), cw.astype(BF16), pick(lbr), pick(lbi)


def _s5_body(*refs, tl, nt, nj, emit_y):
    if emit_y:
        (u_hbm, bw_ref, cw_ref, lr_ref, li_ref, init_ref, y_hbm, fin_ref,
         u_scr, bu_scr, st_scr, in_sem, y_scr, out_sem) = refs
    else:
        (u_hbm, bw_ref, lr_ref, li_ref, init_ref, fin_ref, u_scr, bu_scr, st_scr, in_sem) = refs
    g = pl.program_id(0)
    ng = pl.num_programs(0)
    d = g // nt
    t = g % nt
    slot = g % 2
    nb = SUBLANES
    mpb = MODES_PER_BLOCK
    c = nj * LANES
    rows = tl * nb

    def tile_of(step):
        dd = step // nt
        tt = step % nt
        return dd, tt + dd * (nt - 1 - 2 * tt)

    def in_copies(step, sl):
        _, tile = tile_of(step)
        return [pltpu.make_async_copy(u_hbm.at[b, pl.ds(tile * tl, tl), pl.ds(0, c)],
                                      u_scr.at[sl, :, b, :], in_sem.at[sl]) for b in range(nb)]

    def out_copies(step, sl):
        dd, tile = tile_of(step)
        return [pltpu.make_async_copy(y_scr.at[sl, :, b, :], y_hbm.at[dd, b, pl.ds(tile * tl, tl), :],
                                      out_sem.at[sl]) for b in range(nb)]

    @pl.when(g == 0)
    def _():
        for cp in in_copies(0, 0):
            cp.start()

    @pl.when(g + 1 < ng)
    def _():
        for cp in in_copies(g + 1, (g + 1) % 2):
            cp.start()

    for cp in in_copies(g, slot):
        cp.wait()

    if emit_y:
        @pl.when(g >= 2)
        def _():
            for cp in out_copies(g - 2, slot):
                cp.wait()

    @pl.when(t == 0)
    def _():
        st_scr[...] = init_ref[0]

    def project(j):
        s = u_scr[slot, :, :, j * LANES:(j + 1) * LANES].reshape(rows, LANES)
        bu_scr[j % 3] = _dot(s.astype(BF16), bw_ref[0, j])

    def readout(j):
        y = _dot(bu_scr[j % 3].astype(BF16), cw_ref[0, j])
        y_scr[slot, :, :, j * LANES:(j + 1) * LANES] = y.reshape(tl, nb, LANES)

    project(0)
    for j in range(nj):
        if j + 1 < nj:
            project(j + 1)
        if emit_y and j >= 1:
            readout(j - 1)
        lr = jnp.broadcast_to(lr_ref[0, j], (nb, mpb))
        li = jnp.broadcast_to(li_ref[0, j], (nb, mpb))
        cur = j % 3
        xr = st_scr[j, :, 0:mpb]
        xi = st_scr[j, :, mpb:2 * mpb]
        for s in range(tl):
            r0 = pl.multiple_of((s + d * (tl - 1 - 2 * s)) * nb, nb)
            br = bu_scr[cur, pl.ds(r0, nb), 0:mpb]
            bi = bu_scr[cur, pl.ds(r0, nb), mpb:2 * mpb]
            xr, xi = lr * xr - li * xi + br, lr * xi + li * xr + bi
            bu_scr[cur, pl.ds(r0, nb), 0:mpb] = xr
            bu_scr[cur, pl.ds(r0, nb), mpb:2 * mpb] = xi
        st_scr[j, :, 0:mpb] = xr
        st_scr[j, :, mpb:2 * mpb] = xi

    if emit_y:
        readout(nj - 1)
        for cp in out_copies(g, slot):
            cp.start()

        @pl.when(g == ng - 1)
        def _():
            for cp in out_copies(g - 1, 1 - slot):
                cp.wait()
            for cp in out_copies(g, slot):
                cp.wait()

    @pl.when(t == nt - 1)
    def _():
        fin_ref[0] = st_scr[...]


def _s5_scan(u_arr, bw, cw, lamr, lami, init, tl, emit_y):
    b, l = u_arr.shape[:2]
    assert b == SUBLANES and l % tl == 0
    nj = bw.shape[1]
    c = nj * LANES
    nt = l // tl
    m2 = 2 * MODES_PER_BLOCK
    wmap = lambda g: (g // nt, 0, 0, 0)
    in_specs = [pl.BlockSpec(memory_space=pl.ANY), pl.BlockSpec((1, nj, LANES, m2), wmap)]
    args = [u_arr, bw]
    if emit_y:
        in_specs.append(pl.BlockSpec((1, nj, m2, LANES), wmap))
        args.append(cw)
    in_specs += [pl.BlockSpec((1, nj, 1, MODES_PER_BLOCK), wmap),
                 pl.BlockSpec((1, nj, 1, MODES_PER_BLOCK), wmap),
                 pl.BlockSpec((1, nj, SUBLANES, m2), wmap)]
    args += [lamr, lami, init]
    fin_spec = pl.BlockSpec((1, nj, SUBLANES, m2), wmap)
    fin_shape = jax.ShapeDtypeStruct((2, nj, SUBLANES, m2), F32)
    scratch = [pltpu.VMEM((2, tl, SUBLANES, c), F32),
               pltpu.VMEM((3, tl * SUBLANES, m2), F32),
               pltpu.VMEM((nj, SUBLANES, m2), F32),
               pltpu.SemaphoreType.DMA((2,))]
    if emit_y:
        out_specs = [pl.BlockSpec(memory_space=pl.ANY), fin_spec]
        out_shape = [jax.ShapeDtypeStruct((2, b, l, c), F32), fin_shape]
        scratch += [pltpu.VMEM((2, tl, SUBLANES, c), F32), pltpu.SemaphoreType.DMA((2,))]
    else:
        out_specs = [fin_spec]
        out_shape = [fin_shape]
    return pl.pallas_call(
        functools.partial(_s5_body, tl=tl, nt=nt, nj=nj, emit_y=emit_y),
        grid=(2 * nt,),
        in_specs=in_specs, out_specs=out_specs, out_shape=out_shape,
        scratch_shapes=scratch,
        compiler_params=_cparams(("arbitrary",), 48),
        name="s5scan_lat" if emit_y else "s5scan_ctx",
    )(*args)


def _hyfilt_body(z_ref, w1_ref, b1_ref, w2_ref, b2_ref, w3_ref, b3_ref, fr_ref,
                 w4f_ref, w4b_ref, df_ref, db_ref, ka_ref, kd_ref, h_scr, *, length):
    z = z_ref[...]

    @pl.when(pl.program_id(0) == 0)
    def _():
        fr = fr_ref[...]
        h = jnp.sin(fr * (_dot_hi(z, w1_ref[...]) + b1_ref[...]))
        h = jnp.sin(fr * (_dot_hi(h, w2_ref[...]) + b2_ref[...]))
        h_scr[...] = jnp.sin(fr * (_dot_hi(h, w3_ref[...]) + b3_ref[...]))

    h = h_scr[...]
    t = z[:, 0:1]
    hf = _dot_hi(h, w4f_ref[...]) * jnp.exp(-t * jnp.abs(df_ref[...]))
    hb = _dot_hi(h, w4b_ref[...]) * jnp.exp(-t * jnp.abs(db_ref[...]))
    row = lax.broadcasted_iota(I32, hf.shape, 0)
    hb = jnp.where(row < length - 1, hb, 0.0)
    norm = jnp.sum(jnp.abs(hf), axis=0, keepdims=True) + jnp.sum(jnp.abs(hb), axis=0, keepdims=True)
    kf = hf / norm
    kb = jnp.where(row >= 1, pltpu.roll(hb, 1, axis=0), 0.0) / norm
    ka_ref[...] = kf + kb
    kd_ref[...] = kf - kb


def _hyena_filter(z, w1, b1, w2, b2, w3, b3, fr, w4, deltas, ct=256):
    length, ze = z.shape
    order = w2.shape[0]
    c = w4.shape[1] // 2
    nct = c // ct
    full = lambda shape: pl.BlockSpec(shape, lambda i: (0, 0))
    shp = jax.ShapeDtypeStruct((length, c), F32)
    return pl.pallas_call(
        functools.partial(_hyfilt_body, length=length),
        grid=(nct,),
        in_specs=[full((length, ze)), full((ze, order)), full((1, order)),
                  full((order, order)), full((1, order)), full((order, order)), full((1, order)),
                  full((1, order)),
                  pl.BlockSpec((order, ct), lambda i: (0, i)),
                  pl.BlockSpec((order, ct), lambda i: (0, nct + i)),
                  pl.BlockSpec((1, ct), lambda i: (0, i)),
                  pl.BlockSpec((1, ct), lambda i: (0, nct + i))],
        out_specs=[pl.BlockSpec((length, ct), lambda i: (0, i))] * 2,
        out_shape=[shp, shp],
        scratch_shapes=[pltpu.VMEM((length, order), F32)],
        compiler_params=_cparams(("arbitrary",), 48),
        name="hyena_filter",
    )(z, w1, b1, w2, b2, w3, b3, fr, w4, w4, deltas, deltas)


def _dfttab_body(ca_ref, sa_ref, cb_ref, sb_ref, c_ref, s_ref):
    ca = ca_ref[0]
    sa = sa_ref[0]
    cb = cb_ref[...]
    sb = sb_ref[...]
    c_ref[...] = (ca * cb - sa * sb).astype(c_ref.dtype)
    s_ref[...] = (sa * cb + ca * sb).astype(s_ref.dtype)


def _dft_tables(length, r=128):
    n = 2 * length
    s = jnp.arange(length, dtype=I32)[None, :]
    ang = lambda k: (2.0 * math.pi / n) * ((k * s) % n).astype(F32)
    aa = ang(jnp.arange(length // r, dtype=I32)[:, None] * r)
    ab = ang(jnp.arange(r, dtype=I32)[:, None])
    rows3 = lambda a: a.reshape(length // r, 1, length)
    shp = jax.ShapeDtypeStruct((length, length), BF16)
    return pl.pallas_call(
        _dfttab_body,
        grid=(length // r,),
        in_specs=[pl.BlockSpec((None, 1, length), lambda i: (i, 0, 0)),
                  pl.BlockSpec((None, 1, length), lambda i: (i, 0, 0)),
                  pl.BlockSpec((r, length), lambda i: (0, 0)),
                  pl.BlockSpec((r, length), lambda i: (0, 0))],
        out_specs=[pl.BlockSpec((r, length), lambda i: (i, 0))] * 2,
        out_shape=[shp, shp],
        compiler_params=_cparams(("arbitrary",), 32),
        name="dft_tables",
    )(rows3(jnp.cos(aa)), rows3(jnp.sin(aa)), jnp.cos(ab), jnp.sin(ab))


def _hyspec_body(c_ref, s_ref, ka_ref, kd_ref, kk_ref, kn_ref, *, length, fq):
    q = pl.program_id(1)
    nq = length // fq
    n = 2.0 * length

    @pl.when(q < nq)
    def _():
        grow = q * fq + lax.broadcasted_iota(I32, (fq, 1), 0)
        scale = jnp.where(grow == 0, 1.0 / n, 2.0 / n)
        kk_ref[...] = _dot(c_ref[...], ka_ref[...].astype(BF16)) * scale

    @pl.when(q >= nq)
    def _():
        kk_ref[...] = _dot(s_ref[...], kd_ref[...].astype(BF16)) * (2.0 / n)

    @pl.when(q == 0)
    def _():
        a = ka_ref[...]
        row = lax.broadcasted_iota(I32, a.shape, 0)
        sgn = jnp.where(row % 2 == 0, 1.0, -1.0)
        kn = jnp.sum(a * sgn, axis=0, keepdims=True) * (1.0 / n)
        kn_ref[...] = jnp.broadcast_to(kn, kn_ref.shape)


def _hyena_spectrum(ctab, stab, ka, kd, ct=256, fq=512):
    length, c = ka.shape
    nq = length // fq
    return pl.pallas_call(
        functools.partial(_hyspec_body, length=length, fq=fq),
        grid=(c // ct, 2 * nq),
        in_specs=[pl.BlockSpec((fq, length), lambda i, q: (jnp.minimum(q, nq - 1), 0)),
                  pl.BlockSpec((fq, length), lambda i, q: (jnp.maximum(q - nq, 0), 0)),
                  pl.BlockSpec((length, ct), lambda i, q: (0, i)),
                  pl.BlockSpec((length, ct), lambda i, q: (0, i))],
        out_specs=[pl.BlockSpec((fq, ct), lambda i, q: (q, i)),
                   pl.BlockSpec((SUBLANES, ct), lambda i, q: (0, i))],
        out_shape=[jax.ShapeDtypeStruct((2 * length, c), F32), jax.ShapeDtypeStruct((SUBLANES, c), F32)],
        compiler_params=_cparams(("arbitrary", "arbitrary"), 48),
        name="hyena_spectrum",
    )(ctab, stab, ka, kd)


def _inproj_hy_body(x_ref, g_ref, sh_ref, sc_ref, wv_ref, w1_ref, w0_ref, cv_ref, c1_ref, c0_ref,
                    bv_ref, b1_ref, b0_ref, v_ref, x0_ref):
    half = x_ref.shape[1] // 2
    row = lax.broadcasted_iota(I32, (half, 1), 0) % GRID_W

    def sconv(a, w_ref, b_ref):
        prev = jnp.where(row == 0, 0.0, pltpu.roll(a, 1, axis=0))
        nxt = jnp.where(row == GRID_W - 1, 0.0, pltpu.roll(a, half - 1, axis=0))
        return prev * w_ref[0:1, :] + a * w_ref[1:2, :] + nxt * w_ref[2:3, :] + b_ref[...]

    for r0 in (0, half):
        rs = slice(r0, r0 + half)
        h = _norm_mod(x_ref[0, rs, :], g_ref[...], sh_ref[0], sc_ref[0]).astype(BF16)
        zv = sconv(_dot(h, wv_ref[...]), cv_ref, bv_ref)
        z1 = sconv(_dot(h, w1_ref[...]), c1_ref, b1_ref)
        v_ref[0, rs, :] = (zv * z1).astype(v_ref.dtype)
        x0_ref[0, rs, :] = sconv(_dot(h, w0_ref[...]), c0_ref, b0_ref).astype(x0_ref.dtype)


def _inproj_hyena(x, g, sh, sc, w, conv_w, conv_b, c, first_col, tm=512, ct=256):
    b, l, d = x.shape
    assert tm % (2 * GRID_W) == 0
    nct = c // ct
    base = first_col // ct
    tokmap = lambda ci, bi, li: (bi, li, 0)
    perb = pl.BlockSpec((1, 1, d), lambda ci, bi, li: (bi, 0, 0))
    wspec = lambda k: pl.BlockSpec((d, ct), lambda ci, bi, li: (0, base + k * nct + ci))
    cspec = lambda k: pl.BlockSpec((3, ct), lambda ci, bi, li: (0, k * nct + ci))
    bspec = lambda k: pl.BlockSpec((1, ct), lambda ci, bi, li: (0, k * nct + ci))
    shp = jax.ShapeDtypeStruct((b, l, c), BF16)
    return pl.pallas_call(
        _inproj_hy_body,
        grid=(nct, b, l // tm),
        in_specs=[pl.BlockSpec((1, tm, d), tokmap), pl.BlockSpec((1, d), lambda ci, bi, li: (0, 0)), perb, perb,
                  wspec(0), wspec(1), wspec(2), cspec(0), cspec(1), cspec(2), bspec(0), bspec(1), bspec(2)],
        out_specs=[pl.BlockSpec((1, tm, ct), lambda ci, bi, li: (bi, li, ci))] * 2,
        out_shape=[shp, shp],
        compiler_params=_cparams(("arbitrary", "arbitrary", "arbitrary"), 40),
        name="inproj_hyena",
    )(x, g, sh, sc, w, w, w, conv_w, conv_w, conv_w, conv_b, conv_b, conv_b)


def _hyconv_body(v_ref, x0_ref, kk_ref, kn_ref, d_ref, c_hbm, s_hbm, o_ref, c_scr, s_scr, sem, *, length, fq):
    first = jnp.logical_and(pl.program_id(0) == 0, pl.program_id(1) == 0)

    @pl.when(first)
    def _():
        cc = pltpu.make_async_copy(c_hbm, c_scr, sem.at[0])
        cs = pltpu.make_async_copy(s_hbm, s_scr, sem.at[1])
        cc.start()
        cs.start()
        cc.wait()
        cs.wait()

    vb = v_ref[0]
    v = vb.astype(F32)
    row = lax.broadcasted_iota(I32, (length, 1), 0)
    sgn = jnp.where(row % 2 == 0, 1.0, -1.0)
    vn = jnp.sum(v * sgn, axis=0, keepdims=True)
    acc = sgn * (vn * kn_ref[0:1, :])
    for q in range(length // fq):
        lo, hi = q * fq, (q + 1) * fq
        vr = _dot(c_scr[lo:hi, :], vb)
        vs = _dot(s_scr[lo:hi, :], vb)
        kr = kk_ref[lo:hi, :]
        ks = kk_ref[length + lo:length + hi, :]
        pr = vr * kr - vs * ks
        ps = vr * ks + vs * kr
        acc = acc + _dot(c_scr[:, lo:hi], pr.astype(BF16)) + _dot(s_scr[:, lo:hi], ps.astype(BF16))
    o_ref[0] = ((acc + v * d_ref[...]) * x0_ref[0].astype(F32)).astype(o_ref.dtype)


def _hyena_conv(v, x0, kk, kn, d_skip, ctab, stab, ct=256, fq=512):
    b, l, c = v.shape
    return pl.pallas_call(
        functools.partial(_hyconv_body, length=l, fq=fq),
        grid=(c // ct, b),
        in_specs=[pl.BlockSpec((1, l, ct), lambda ci, bi: (bi, 0, ci)),
                  pl.BlockSpec((1, l, ct), lambda ci, bi: (bi, 0, ci)),
                  pl.BlockSpec((2 * l, ct), lambda ci, bi: (0, ci)),
                  pl.BlockSpec((SUBLANES, ct), lambda ci, bi: (0, ci)),
                  pl.BlockSpec((1, ct), lambda ci, bi: (0, ci)),
                  pl.BlockSpec(memory_space=pl.ANY),
                  pl.BlockSpec(memory_space=pl.ANY)],
        out_specs=pl.BlockSpec((1, l, ct), lambda ci, bi: (bi, 0, ci)),
        out_shape=jax.ShapeDtypeStruct((b, l, c), BF16),
        scratch_shapes=[pltpu.VMEM((l, l), BF16), pltpu.VMEM((l, l), BF16),
                        pltpu.SemaphoreType.DMA((2,))],
        compiler_params=_cparams(("arbitrary", "arbitrary"), 48),
        name="hyena_conv",
    )(v, x0, kk, kn, d_skip, ctab, stab)


def _split_bf16(a):
    hi = a.astype(BF16)
    return hi, (a - hi.astype(F32)).astype(BF16)


def _outproj_body(y_ref, u_ref, yh_ref, x_ref, d_ref, wg_ref, bg_ref, wo_ref, ga_ref, g2_ref,
                  sh_ref, sc_ref, wrh_ref, wrl_ref, br_ref, x1_ref, hp_ref, te_ref, gt_ref, cnt_ref, *, sub):
    tm, c = u_ref.shape[1], u_ref.shape[2]

    @pl.when(jnp.logical_and(pl.program_id(0) == 0, pl.program_id(1) == 0))
    def _():
        cnt_ref[...] = jnp.zeros_like(cnt_ref)

    lane = lax.broadcasted_iota(I32, (sub, LANES), 1)
    total = jnp.zeros((1, LANES), F32)
    for r0 in range(0, tm, sub):
        rs = slice(r0, r0 + sub)
        ys = y_ref[0, 0, rs, :] + y_ref[1, 0, rs, :] + u_ref[0, rs, :] * d_ref[...]
        ge = 0.5 * ys * (1.0 + lax.erf(ys * (1.0 / math.sqrt(2.0))))
        gl = ge * jax.nn.sigmoid(_dot(ge.astype(BF16), wg_ref[...]) + bg_ref[...])
        mix = _dot(gl.astype(BF16), wo_ref[0:c, :]) + _dot(yh_ref[0, rs, :], wo_ref[c:2 * c, :])
        x1 = x_ref[0, rs, :] + ga_ref[0] * mix
        x1_ref[0, rs, :] = x1
        h2 = _norm_mod(x1, g2_ref[...], sh_ref[0], sc_ref[0])
        hp_ref[0, rs, :] = h2

        hh, hl = _split_bf16(h2)
        logits = _dot(hh, wrh_ref[...]) + (_dot(hh, wrl_ref[...]) + _dot(hl, wrh_ref[...])) + br_ref[...]
        work = logits
        vals, idxs = [], []
        for _ in range(TOP_K):
            m = jnp.max(work, axis=-1, keepdims=True)
            ix = jnp.min(jnp.where(work == m, lane, LANES), axis=-1, keepdims=True)
            vals.append(m)
            idxs.append(ix)
            work = jnp.where(lane == ix, NEG_BIG, work)
        ex = [jnp.exp(v - vals[0]) for v in vals]
        den = ex[0]
        for e in ex[1:]:
            den = den + e
        te = jnp.zeros((sub, LANES), I32)
        gt = jnp.zeros((sub, LANES), F32)
        mh = jnp.zeros((sub, LANES), F32)
        for k in range(TOP_K):
            te = jnp.where(lane == k, idxs[k], te)
            gt = jnp.where(lane == k, ex[k] / den, gt)
            mh = mh + (lane == idxs[k]).astype(F32)
        te_ref[0, rs, :] = te
        gt_ref[0, rs, :] = gt
        total = total + jnp.sum(mh, axis=0, keepdims=True)

    cnt_ref[...] += jnp.broadcast_to(total, cnt_ref.shape)


def _outproj(y, p, yh, x, d_skip, w_glu, b_glu, w_out, ga1, g2, sh2, sc2, w_rh, w_rl, b_r, tm=256, sub=128):
    b, l, dm = x.shape
    c = yh.shape[2]
    tok = lambda width: pl.BlockSpec((1, tm, width), lambda bi, li: (bi, li, 0))
    full = lambda shape: pl.BlockSpec(shape, lambda bi, li: (0,) * len(shape))
    perb = pl.BlockSpec((1, 1, dm), lambda bi, li: (bi, 0, 0))
    return pl.pallas_call(
        functools.partial(_outproj_body, sub=sub),
        grid=(b, l // tm),
        in_specs=[pl.BlockSpec((2, 1, tm, c), lambda bi, li: (0, bi, li, 0)),
                  tok(c), tok(c), tok(dm),
                  full((1, c)), full((c, c)), full((1, c)), full((2 * c, dm)),
                  perb, full((1, dm)), perb, perb,
                  full((dm, LANES)), full((dm, LANES)), full((1, LANES))],
        out_specs=[tok(dm), tok(dm), tok(LANES), tok(LANES), full((SUBLANES, LANES))],
        out_shape=[jax.ShapeDtypeStruct((b, l, dm), F32),
                   jax.ShapeDtypeStruct((b, l, dm), F32),
                   jax.ShapeDtypeStruct((b, l, LANES), I32),
                   jax.ShapeDtypeStruct((b, l, LANES), F32),
                   jax.ShapeDtypeStruct((SUBLANES, LANES), F32)],
        compiler_params=_cparams(("arbitrary", "arbitrary"), 56),
        name="outproj_router",
    )(y, p, yh, x, d_skip, w_glu, b_glu, w_out, ga1, g2, sh2, sc2, w_rh, w_rl, b_r)


def _slots_body(te_ref, seg_ref, pos_ref, carry):
    @pl.when(pl.program_id(0) == 0)
    def _():
        carry[...] = jnp.zeros_like(carry)

    te = te_ref[...]
    tm = te.shape[0]
    lane = lax.broadcasted_iota(I32, (tm, LANES), 1)
    sel = [lane == te[:, k:k + 1] for k in range(TOP_K)]
    mh = jnp.zeros((tm, LANES), F32)
    for s in sel:
        mh = mh + s.astype(F32)
    r = lax.broadcasted_iota(I32, (tm, tm), 0)
    cc = lax.broadcasted_iota(I32, (tm, tm), 1)
    strict_lower = (r > cc).astype(BF16)
    slot = _dot(strict_lower, mh.astype(BF16)) + carry[0:1, :] + seg_ref[0:1, :]
    pos = jnp.zeros((tm, LANES), F32)
    for k in range(TOP_K):
        pk = jnp.sum(jnp.where(sel[k], slot, 0.0), axis=-1, keepdims=True)
        pos = jnp.where(lane == k, pk, pos)
    pos_ref[...] = pos.astype(I32)
    carry[...] += jnp.broadcast_to(jnp.sum(mh, axis=0, keepdims=True), carry.shape)


def _slots(te, seg_start, tm=512):
    t = te.shape[0]
    return pl.pallas_call(
        _slots_body,
        grid=(t // tm,),
        in_specs=[pl.BlockSpec((tm, LANES), lambda i: (i, 0)),
                  pl.BlockSpec((SUBLANES, LANES), lambda i: (0, 0))],
        out_specs=pl.BlockSpec((tm, LANES), lambda i: (i, 0)),
        out_shape=jax.ShapeDtypeStruct((t, LANES), I32),
        scratch_shapes=[pltpu.VMEM((SUBLANES, LANES), F32)],
        compiler_params=_cparams(("arbitrary",), 32),
        name="moe_slots",
    )(te, seg_start)


PAD_CHUNKS = tuple(s for s in (ROW_BLOCK >> k for k in range(1, ROW_BLOCK.bit_length())) if s >= SUBLANES)


def _row_copy(src, src_row, dst, dst_row, sem):
    return pltpu.make_async_copy(src.at[pl.ds(src_row, 1)], dst.at[pl.ds(dst_row, 1)], sem)


def _dispatch_body(pos_ref, ps_ref, pn_ref, h_ref, xs_out, zbuf, sem, zsem, *, tm, n_exp, n_blocks):
    @pl.when(pl.program_id(0) == 0)
    def _():
        zbuf[...] = jnp.zeros_like(zbuf)

        def copies(e):
            start = ps_ref[e]
            head = (-start) & (SUBLANES - 1)
            out = [(r < head, _row_copy(zbuf, 0, xs_out, start + r, zsem)) for r in range(SUBLANES - 1)]
            body = pn_ref[e] - head
            for size in PAD_CHUNKS:
                done = body & (-2 * size)
                dst = xs_out.at[pl.ds(pl.multiple_of(start + head + done, SUBLANES), size)]
                out.append(((body & size) != 0, pltpu.make_async_copy(zbuf.at[pl.ds(0, size)], dst, zsem)))
            return out

        def fill(e, carry):
            for on, cp in copies(e):
                pl.when(on)(cp.start)
            return carry

        def drain(e, carry):
            for on, cp in copies(e):
                pl.when(on)(cp.wait)
            return carry

        lax.fori_loop(0, n_exp, fill, 0)
        lax.fori_loop(0, n_exp, drain, 0)

        def tail(blk):
            rows = [pl.multiple_of(blk * ROW_BLOCK + part * PAD_CHUNKS[0], SUBLANES)
                    for part in range(ROW_BLOCK // PAD_CHUNKS[0])]
            return [pltpu.make_async_copy(zbuf, xs_out.at[pl.ds(r, PAD_CHUNKS[0])], zsem) for r in rows]

        def fill_tail(blk, carry):
            for cp in tail(blk):
                cp.start()
            return carry

        def drain_tail(blk, carry):
            for cp in tail(blk):
                cp.wait()
            return carry

        first_unused = (ps_ref[n_exp - 1] + pn_ref[n_exp - 1]) // ROW_BLOCK
        lax.fori_loop(first_unused, n_blocks, fill_tail, 0)
        lax.fori_loop(first_unused, n_blocks, drain_tail, 0)

    base = pl.program_id(0) * (tm * TOP_K)

    def issue(r, carry):
        for k in range(TOP_K):
            _row_copy(h_ref, r, xs_out, pos_ref[base + r * TOP_K + k], sem).start()
        return carry

    lax.fori_loop(0, tm, issue, 0, unroll=8)

    def drain_rows(r, carry):
        for k in range(TOP_K):
            _row_copy(h_ref, 0, xs_out, 0, sem).wait()
        return carry

    lax.fori_loop(0, tm, drain_rows, 0, unroll=8)


def _dispatch(pos_flat, pad_start, pad_len, hp, n_rows, tm=256):
    t, w = hp.shape
    return pl.pallas_call(
        functools.partial(_dispatch_body, tm=tm, n_exp=pad_start.shape[0], n_blocks=n_rows // ROW_BLOCK),
        grid_spec=pltpu.PrefetchScalarGridSpec(
            num_scalar_prefetch=3,
            grid=(t // tm,),
            in_specs=[pl.BlockSpec((tm, w), lambda i, pos, ps, pn: (i, 0))],
            out_specs=pl.BlockSpec(memory_space=pl.ANY),
            scratch_shapes=[pltpu.VMEM((PAD_CHUNKS[0], w), hp.dtype),
                            pltpu.SemaphoreType.DMA(()), pltpu.SemaphoreType.DMA(())]),
        out_shape=jax.ShapeDtypeStruct((n_rows, w), hp.dtype),
        compiler_params=_cparams(("arbitrary",), 32),
        name="moe_dispatch",
    )(pos_flat, pad_start, pad_len, hp)


def _stream_expert_blocks(first, count, total, n_blocks, src_hbm, dst_hbm, col0, ibuf, obuf, isem, osem, compute):
    e = pl.program_id(1)
    rb = ROW_BLOCK
    n_in = ibuf.shape[0]
    width = obuf.shape[2]

    def rows(gs):
        return pl.ds(pl.multiple_of(gs * rb, rb), rb)

    def in_copy(gs):
        slot = gs % n_in
        return pltpu.make_async_copy(src_hbm.at[rows(gs)], ibuf.at[slot], isem.at[slot])

    def out_copy(gs, slot):
        return pltpu.make_async_copy(obuf.at[slot], dst_hbm.at[rows(gs), pl.ds(col0, width)], osem.at[slot])

    @pl.when(e == 0)
    def _():
        for gs in range(n_in - 1):
            pl.when(gs < total)(in_copy(gs).start)

    def block(s, carry):
        gs = first + s
        slot = gs % 2

        @pl.when(gs + n_in - 1 < total)
        def _():
            in_copy(gs + n_in - 1).start()

        in_copy(gs).wait()

        @pl.when(gs >= 2)
        def _():
            out_copy(gs - 2, slot).wait()

        compute(ibuf.at[gs % n_in], obuf.at[slot])
        out_copy(gs, slot).start()
        return carry

    lax.fori_loop(0, count, block, 0)

    @pl.when(e == pl.num_programs(1) - 1)
    def _():
        @pl.when(total >= 2)
        def _():
            out_copy(total - 2, total % 2).wait()

        out_copy(total - 1, (total - 1) % 2).wait()
        obuf[0] = jnp.zeros(obuf.shape[1:], obuf.dtype)

        def fill(gs, carry):
            out_copy(gs, 0).start()
            return carry

        def drain(gs, carry):
            out_copy(gs, 0).wait()
            return carry

        lax.fori_loop(total, n_blocks, fill, 0)
        lax.fori_loop(total, n_blocks, drain, 0)


def _gateup_body(fs_ref, ns_ref, tot_ref, xs_hbm, wg_ref, wu_ref, bg_ref, bu_ref, h_hbm,
                 wg_s, wu_s, xbuf, obuf, xsem, osem, *, n_blocks):
    e = pl.program_id(1)
    tn = obuf.shape[2]
    half = tn // 2

    @pl.when(ns_ref[e] > 0)
    def _():
        wg_s[...] = wg_ref[...].astype(BF16)
        wu_s[...] = wu_ref[...].astype(BF16)

    def compute(x_ref, o_ref):
        x = x_ref[...].astype(BF16)
        for c0 in (0, half):
            cs = slice(c0, c0 + half)
            g = _dot(x, wg_s[:, cs]) + bg_ref[:, cs]
            u = _dot(x, wu_s[:, cs]) + bu_ref[:, cs]
            g = jnp.minimum(g, SWIGLU_LIMIT)
            u = jnp.clip(u, -SWIGLU_LIMIT, SWIGLU_LIMIT)
            o_ref[:, cs] = ((u + 1.0) * (g * jax.nn.sigmoid(SWIGLU_ALPHA * g))).astype(o_ref.dtype)

    col0 = pl.multiple_of(pl.program_id(0) * tn, tn)
    _stream_expert_blocks(fs_ref[e], ns_ref[e], tot_ref[0], n_blocks, xs_hbm, h_hbm, col0,
                          xbuf, obuf, xsem, osem, compute)


def _expert_gateup(first_blk, n_blk, total, xs, w_gate_up, b_gate_up, tn=1024):
    n_exp, dm, ff2 = w_gate_up.shape
    ff = ff2 // 2
    nr = xs.shape[0]
    nn = ff // tn
    return pl.pallas_call(
        functools.partial(_gateup_body, n_blocks=nr // ROW_BLOCK),
        grid_spec=pltpu.PrefetchScalarGridSpec(
            num_scalar_prefetch=3,
            grid=(nn, n_exp),
            in_specs=[pl.BlockSpec(memory_space=pl.ANY),
                      pl.BlockSpec((None, dm, tn), lambda j, e, fs, ns, tot: (e, 0, j)),
                      pl.BlockSpec((None, dm, tn), lambda j, e, fs, ns, tot: (e, 0, nn + j)),
                      pl.BlockSpec((None, 1, tn), lambda j, e, fs, ns, tot: (e, 0, j)),
                      pl.BlockSpec((None, 1, tn), lambda j, e, fs, ns, tot: (e, 0, nn + j))],
            out_specs=pl.BlockSpec(memory_space=pl.ANY),
            scratch_shapes=[pltpu.VMEM((dm, tn), BF16), pltpu.VMEM((dm, tn), BF16),
                            pltpu.VMEM((IN_SLOTS, ROW_BLOCK, dm), xs.dtype), pltpu.VMEM((2, ROW_BLOCK, tn), BF16),
                            pltpu.SemaphoreType.DMA((IN_SLOTS,)), pltpu.SemaphoreType.DMA((2,))]),
        out_shape=jax.ShapeDtypeStruct((nr, ff), BF16),
        compiler_params=_cparams(("arbitrary", "arbitrary"), 56),
        name="moe_gateup",
    )(first_blk, n_blk, total, xs, w_gate_up, w_gate_up, b_gate_up, b_gate_up)


def _down_body(fs_ref, ns_ref, tot_ref, h_hbm, w_ref, b_ref, y_hbm, w_s, hbuf, obuf, hsem, osem, *, n_blocks):
    e = pl.program_id(1)
    tn = obuf.shape[2]

    @pl.when(ns_ref[e] > 0)
    def _():
        w_s[...] = w_ref[...].astype(BF16)

    def compute(h_ref, o_ref):
        o_ref[...] = _dot(h_ref[...], w_s[...]) + b_ref[...]

    col0 = pl.multiple_of(pl.program_id(0) * tn, tn)
    _stream_expert_blocks(fs_ref[e], ns_ref[e], tot_ref[0], n_blocks, h_hbm, y_hbm, col0,
                          hbuf, obuf, hsem, osem, compute)


def _expert_down(first_blk, n_blk, total, h, w_down, b_down, tn=2048):
    n_exp, ff, dm = w_down.shape
    nr = h.shape[0]
    return pl.pallas_call(
        functools.partial(_down_body, n_blocks=nr // ROW_BLOCK),
        grid_spec=pltpu.PrefetchScalarGridSpec(
            num_scalar_prefetch=3,
            grid=(dm // tn, n_exp),
            in_specs=[pl.BlockSpec(memory_space=pl.ANY),
                      pl.BlockSpec((None, ff, tn), lambda j, e, fs, ns, tot: (e, 0, j)),
                      pl.BlockSpec((None, 1, tn), lambda j, e, fs, ns, tot: (e, 0, j))],
            out_specs=pl.BlockSpec(memory_space=pl.ANY),
            scratch_shapes=[pltpu.VMEM((ff, tn), BF16),
                            pltpu.VMEM((IN_SLOTS, ROW_BLOCK, ff), h.dtype), pltpu.VMEM((2, ROW_BLOCK, tn), F32),
                            pltpu.SemaphoreType.DMA((IN_SLOTS,)), pltpu.SemaphoreType.DMA((2,))]),
        out_shape=jax.ShapeDtypeStruct((nr, dm), F32),
        compiler_params=_cparams(("arbitrary", "arbitrary"), 56),
        name="moe_down",
    )(first_blk, n_blk, total, h, w_down, b_down)


def _combine_body(pos_ref, x1_ref, gt_ref, ga_ref, gf_ref, y_hbm, o_ref, ybuf, sem, *, tm):
    i = pl.program_id(0)
    n = pl.num_programs(0)

    def gather(tile, slot):
        def issue(r, carry):
            for k in range(TOP_K):
                p = pos_ref[(tile * tm + r) * TOP_K + k]
                _row_copy(y_hbm, p, ybuf.at[slot, k], r, sem.at[slot]).start()
            return carry
        lax.fori_loop(0, tm, issue, 0, unroll=8)

    @pl.when(i == 0)
    def _():
        gather(0, 0)

    @pl.when(i + 1 < n)
    def _():
        gather(i + 1, (i + 1) % 2)

    slot = i % 2

    def drain(r, carry):
        for k in range(TOP_K):
            _row_copy(y_hbm, 0, ybuf.at[slot, k], 0, sem.at[slot]).wait()
        return carry

    lax.fori_loop(0, tm, drain, 0, unroll=8)

    gt = gt_ref[0]
    acc = gt[:, 0:1] * ybuf[slot, 0]
    for k in range(1, TOP_K):
        acc = acc + gt[:, k:k + 1] * ybuf[slot, k]
    x2 = x1_ref[0] + ga_ref[0] * acc
    ms = jnp.mean(x2 * x2, axis=-1, keepdims=True)
    o_ref[0] = x2 * lax.rsqrt(ms + EPS) * gf_ref[...]


def _combine(pos_flat, x1, gate, ga2, g_final, y, tm=128):
    b, l, dm = x1.shape
    per = l // tm
    tok = lambda width: pl.BlockSpec((1, tm, width), lambda i, pos: (i // per, i % per, 0))
    return pl.pallas_call(
        functools.partial(_combine_body, tm=tm),
        grid_spec=pltpu.PrefetchScalarGridSpec(
            num_scalar_prefetch=1,
            grid=(b * per,),
            in_specs=[tok(dm), tok(LANES),
                      pl.BlockSpec((1, 1, dm), lambda i, pos: (i // per, 0, 0)),
                      pl.BlockSpec((1, dm), lambda i, pos: (0, 0)),
                      pl.BlockSpec(memory_space=pl.ANY)],
            out_specs=tok(dm),
            scratch_shapes=[pltpu.VMEM((2, TOP_K, tm, dm), F32), pltpu.SemaphoreType.DMA((2,))]),
        out_shape=jax.ShapeDtypeStruct((b, l, dm), F32),
        compiler_params=_cparams(("arbitrary",), 48),
        name="moe_combine",
    )(pos_flat, x1, gate, ga2, g_final, y)


def _filter_features(length):
    t = jnp.linspace(0.0, 1.0, length, dtype=F32)[:, None]
    bands = jnp.linspace(1e-4, HY_BANDS - 1, HY_BANDS, dtype=F32)
    ang = (2 * math.pi / length) * jnp.arange(length, dtype=F32)[:, None] * bands
    z = jnp.concatenate([t, jnp.cos(ang), -jnp.sin(ang)], axis=-1)
    return jnp.pad(z, ((0, 0), (0, LANES - z.shape[1])))


def _segments(counts):
    padded = (counts + ROW_BLOCK - 1) // ROW_BLOCK * ROW_BLOCK
    seg_end = jnp.cumsum(padded)
    seg_start = seg_end - padded
    total = (seg_end[-1:] // ROW_BLOCK).astype(I32)
    return (seg_start, (seg_start // ROW_BLOCK).astype(I32), (padded // ROW_BLOCK).astype(I32), total,
            seg_start + counts, padded - counts)


def kernel(x, c, ctx, c_ctx, w_ada, b_ada, g_norm1, g_norm2, w_in, s5_lam_re, s5_lam_im, s5_log_step,
           s5_b_re, s5_b_im, s5_c_re, s5_c_im, s5_d, s5_w_glu, s5_b_glu, hy_conv_w, hy_conv_b,
           hy_w1, hy_b1, hy_w2, hy_b2, hy_w3, hy_b3, hy_w4, hy_freq, hy_deltas, hy_d, w_out,
           w_router, b_router, w_gate_up, b_gate_up, w_down, b_down, g_final):
    assert w_ada.shape[0] == 1, "single-layer block"
    b, l, dm = x.shape
    lc = ctx.shape[1]
    s5w = s5_d.shape[1]
    hyw = hy_d.shape[1]
    n_exp = w_router.shape[2]
    row = lambda a: a.reshape(1, -1)

    cc = jnp.concatenate([c, c_ctx[None], jnp.zeros((2 * SUBLANES - b - 1, dm), F32)], axis=0)
    mod = _adaln(cc, w_ada[0], row(b_ada[0]))
    sh1, sc1, ga1, sh2, sc2, ga2 = [mod[:b, k * dm:(k + 1) * dm].reshape(b, 1, dm) for k in range(N_MOD)]
    csh1, csc1 = [mod[b:b + 1, k * dm:(k + 1) * dm].reshape(1, 1, dm) for k in range(2)]

    w_in_b = w_in[0].astype(BF16)
    g1 = row(g_norm1[0])
    p = _inproj(x, g1, sh1, sc1, w_in_b, s5w, tm=512)
    p_ctx = _inproj(ctx, g1, csh1, csc1, w_in_b, s5w, tm=lc)
    v, x0 = _inproj_hyena(x, g1, sh1, sc1, w_in_b, hy_conv_w[0], row(hy_conv_b[0]), hyw, s5w)

    bw, cw, lamr, lami = _s5_weights(s5_lam_re[0], s5_lam_im[0], s5_log_step[0],
                                     s5_b_re[0], s5_b_im[0], s5_c_re[0], s5_c_im[0])
    zero_state = jnp.zeros((2, bw.shape[1], SUBLANES, 2 * MODES_PER_BLOCK), F32)
    (ctx_state,) = _s5_scan(p_ctx, bw, cw, lamr, lami, zero_state, tl=64, emit_y=False)
    y_s5, _ = _s5_scan(p, bw, cw, lamr, lami, ctx_state, tl=64, emit_y=True)

    ka, kd = _hyena_filter(_filter_features(l), jnp.pad(hy_w1[0], ((0, LANES - hy_w1.shape[1]), (0, 0))),
                           row(hy_b1[0]), hy_w2[0], row(hy_b2[0]), hy_w3[0], row(hy_b3[0]),
                           row(hy_freq[0]), hy_w4[0], row(hy_deltas[0]))
    ctab, stab = _dft_tables(l)
    kk, kn = _hyena_spectrum(ctab, stab, ka, kd)
    y_hy = _hyena_conv(v, x0, kk, kn, row(hy_d[0]), ctab, stab)

    w_rh, w_rl = _split_bf16(jnp.pad(w_router[0], ((0, 0), (0, LANES - n_exp))))
    b_r = jnp.concatenate([b_router[0], jnp.full((LANES - n_exp,), NEG_BIG, F32)]).reshape(1, LANES)
    x1, h2, te, gate, cnt = _outproj(y_s5, p, y_hy, x, row(s5_d[0]), s5_w_glu[0].astype(BF16),
                                     row(s5_b_glu[0]), w_out[0].astype(BF16), ga1, row(g_norm2[0]),
                                     sh2, sc2, w_rh, w_rl, b_r)

    t = b * l
    n_blocks = t * TOP_K // ROW_BLOCK + n_exp
    counts = cnt[0, :n_exp].astype(I32)
    seg_start, first_blk, n_blk, total, pad_start, pad_len = _segments(counts)
    seg_row = jnp.broadcast_to(jnp.pad(seg_start.astype(F32), (0, LANES - n_exp)), (SUBLANES, LANES))
    pos = _slots(te.reshape(t, LANES), seg_row)
    pos_flat = pos[:, :TOP_K].reshape(-1)
    xs = _dispatch(pos_flat, pad_start, pad_len, h2.reshape(t, dm), n_blocks * ROW_BLOCK)
    hid = _expert_gateup(first_blk, n_blk, total, xs, w_gate_up[0], b_gate_up[0].reshape(n_exp, 1, -1))
    y_e = _expert_down(first_blk, n_blk, total, hid, w_down[0], b_down[0].reshape(n_exp, 1, -1))
    return _combine(pos_flat, x1, gate, ga2, row(g_final), y_e)
```

```python
import functools
import math

import jax
import jax.numpy as jnp
from jax import lax
from jax.experimental import pallas as pl
from jax.experimental.pallas import tpu as pltpu

F32 = jnp.float32
BF16 = jnp.bfloat16
I32 = jnp.int32
U32 = jnp.uint32
HIGHEST = lax.Precision.HIGHEST

GRID_W = 64
S5_H = 16
S5_P = 64
S5_RE_MAX = -1e-4
HY_BANDS = 16
N_EXPERTS = 32
TOP_K = 4
SWIGLU_LIMIT = 7.0
SWIGLU_ALPHA = 1.702
N_MOD = 6
EPS = 1e-6

LANES = 128
SUBLANES = 8
GROUPS_PER_BLOCK = LANES // S5_H
MODES_PER_BLOCK = GROUPS_PER_BLOCK * S5_P
ROW_BLOCK = 256
IN_SLOTS = 3
DMA_SPLIT = 4
NEG_BIG = -1e30


def _cparams(sem, vmem_mb):
    return pltpu.CompilerParams(dimension_semantics=sem, vmem_limit_bytes=vmem_mb << 20)


def _dot(a, b):
    return jnp.dot(a, b, preferred_element_type=F32)


def _dot_hi(a, b):
    return jnp.dot(a, b, preferred_element_type=F32, precision=HIGHEST)


def _adaln_body(c_ref, w_ref, b_ref, o_ref):
    c = c_ref[...]
    s = c * jax.nn.sigmoid(c)
    o_ref[...] = _dot_hi(s, w_ref[...]) + b_ref[...]


def _adaln(cc, w, b, tn=1024):
    rows, d = cc.shape
    n = w.shape[1]
    return pl.pallas_call(
        _adaln_body,
        grid=(n // tn,),
        in_specs=[pl.BlockSpec((rows, d), lambda j: (0, 0)),
                  pl.BlockSpec((d, tn), lambda j: (0, j)),
                  pl.BlockSpec((1, tn), lambda j: (0, j))],
        out_specs=pl.BlockSpec((rows, tn), lambda j: (0, j)),
        out_shape=jax.ShapeDtypeStruct((rows, n), F32),
        compiler_params=_cparams(("arbitrary",), 40),
        name="adaln",
    )(cc, w, b)


def _norm_mod(x, g, sh, sc):
    ms = jnp.mean(x * x, axis=-1, keepdims=True)
    return (x * lax.rsqrt(ms + EPS) * g) * (1.0 + sc) + sh


def _inproj_body(x_ref, g_ref, sh_ref, sc_ref, w_ref, o_ref):
    half = x_ref.shape[1] // 2
    for r0 in (0, half):
        rs = slice(r0, r0 + half)
        h = _norm_mod(x_ref[0, rs, :], g_ref[...], sh_ref[0], sc_ref[0])
        o_ref[0, rs, :] = _dot(h.astype(BF16), w_ref[...])


def _inproj(x, g, sh, sc, w, n_cols, tm, tn=1024):
    b, l, d = x.shape
    if sh.shape[0] > 1:
        bidx = lambda n, bi, li: (bi, 0, 0)
    else:
        bidx = lambda n, bi, li: (0, 0, 0)
    return pl.pallas_call(
        _inproj_body,
        grid=(n_cols // tn, b, l // tm),
        in_specs=[pl.BlockSpec((1, tm, d), lambda n, bi, li: (bi, li, 0)),
                  pl.BlockSpec((1, d), lambda n, bi, li: (0, 0)),
                  pl.BlockSpec((1, 1, d), bidx),
                  pl.BlockSpec((1, 1, d), bidx),
                  pl.BlockSpec((d, tn), lambda n, bi, li: (0, n))],
        out_specs=pl.BlockSpec((1, tm, tn), lambda n, bi, li: (bi, li, n)),
        out_shape=jax.ShapeDtypeStruct((b, l, n_cols), F32),
        compiler_params=_cparams(("arbitrary", "arbitrary", "arbitrary"), 40),
        name="inproj",
    )(x, g, sh, sc, w)


def _s5prep_body(lre_ref, lim_ref, ls_ref, bre_ref, bim_ref, lbr_ref, lbi_ref, bbr_ref, bbi_ref):
    lr = jnp.minimum(lre_ref[...], S5_RE_MAX)
    li = lim_ref[...]
    dt = jnp.exp(ls_ref[...])
    er = jnp.exp(lr * dt)
    lbr = er * jnp.cos(li * dt)
    lbi = er * jnp.sin(li * dt)
    nr = lbr - 1.0
    den = lr * lr + li * li
    qr = (nr * lr + lbi * li) / den
    qi = (lbi * lr - nr * li) / den
    bre = bre_ref[...]
    bim = bim_ref[...]
    lbr_ref[...] = lbr
    lbi_ref[...] = lbi
    bbr_ref[...] = qr * bre - qi * bim
    bbi_ref[...] = qr * bim + qi * bre


def _s5_weights(lam_re, lam_im, log_step, b_re, b_im, c_re, c_im):
    nd, g, p = lam_re.shape
    h = S5_H
    nj = g // GROUPS_PER_BLOCK
    rep = lambda a: jnp.repeat(a.reshape(nd * g, 1, p), h, axis=1).reshape(nd * g * h, p)
    tr = lambda a: jnp.transpose(a, (0, 1, 3, 2)).reshape(nd * g * h, p)
    shp = jax.ShapeDtypeStruct((nd * g * h, p), F32)
    lbr, lbi, bbr, bbi = pl.pallas_call(
        _s5prep_body, out_shape=(shp, shp, shp, shp), name="s5prep",
    )(rep(lam_re), rep(lam_im), rep(log_step), tr(b_re), tr(b_im))
    eye = jnp.eye(GROUPS_PER_BLOCK, dtype=F32)
    bb = jnp.stack([bbr, bbi]).reshape(2, nd, nj, GROUPS_PER_BLOCK, h, p)
    bw = jnp.einsum('rdjahp,ab->djahrbp', bb, eye).reshape(nd, nj, LANES, 2 * MODES_PER_BLOCK)
    cc = jnp.stack([c_re, -c_im]).reshape(2, nd, nj, GROUPS_PER_BLOCK, h, p)
    cw = jnp.einsum('rdjahp,ab->djrapbh', cc, eye).reshape(nd, nj, 2 * MODES_PER_BLOCK, LANES)
    pick = lambda a: a.reshape(nd, g, h, p)[:, :, 0, :].reshape(nd, nj, 1, MODES_PER_BLOCK)
    return bw.astype(BF16), cw.astype(BF16), pick(lbr), pick(lbi)


def _s5_body(*refs, tl, nt, nj, emit_y):
    if emit_y:
        (u_hbm, bw_ref, cw_ref, lr_ref, li_ref, init_ref, y_hbm, fin_ref,
         u_scr, bu_scr, st_scr, in_sem, y_scr, out_sem) = refs
    else:
        (u_hbm, bw_ref, lr_ref, li_ref, init_ref, fin_ref, u_scr, bu_scr, st_scr, in_sem) = refs
    g = pl.program_id(0)
    ng = pl.num_programs(0)
    d = g // nt
    t = g % nt
    slot = g % 2
    nb = SUBLANES
    mpb = MODES_PER_BLOCK
    c = nj * LANES
    rows = tl * nb

    def tile_of(step):
        dd = step // nt
        tt = step % nt
        return dd, tt + dd * (nt - 1 - 2 * tt)

    def in_copies(step, sl):
        _, tile = tile_of(step)
        return [pltpu.make_async_copy(u_hbm.at[b, pl.ds(tile * tl, tl), pl.ds(0, c)],
                                      u_scr.at[sl, :, b, :], in_sem.at[sl]) for b in range(nb)]

    def out_copies(step, sl):
        dd, tile = tile_of(step)
        return [pltpu.make_async_copy(y_scr.at[sl, :, b, :], y_hbm.at[dd, b, pl.ds(tile * tl, tl), :],
                                      out_sem.at[sl]) for b in range(nb)]

    @pl.when(g == 0)
    def _():
        for cp in in_copies(0, 0):
            cp.start()

    @pl.when(g + 1 < ng)
    def _():
        for cp in in_copies(g + 1, (g + 1) % 2):
            cp.start()

    for cp in in_copies(g, slot):
        cp.wait()

    if emit_y:
        @pl.when(g >= 2)
        def _():
            for cp in out_copies(g - 2, slot):
                cp.wait()

    @pl.when(t == 0)
    def _():
        st_scr[...] = init_ref[0]

    def project(j):
        s = u_scr[slot, :, :, j * LANES:(j + 1) * LANES].reshape(rows, LANES)
        bu_scr[j % 3] = _dot(s.astype(BF16), bw_ref[0, j])

    def readout(j):
        y = _dot(bu_scr[j % 3].astype(BF16), cw_ref[0, j])
        y_scr[slot, :, :, j * LANES:(j + 1) * LANES] = y.reshape(tl, nb, LANES)

    project(0)
    for j in range(nj):
        if j + 1 < nj:
            project(j + 1)
        if emit_y and j >= 1:
            readout(j - 1)
        lr = jnp.broadcast_to(lr_ref[0, j], (nb, mpb))
        li = jnp.broadcast_to(li_ref[0, j], (nb, mpb))
        cur = j % 3
        xr = st_scr[j, :, 0:mpb]
        xi = st_scr[j, :, mpb:2 * mpb]
        for s in range(tl):
            r0 = pl.multiple_of((s + d * (tl - 1 - 2 * s)) * nb, nb)
            br = bu_scr[cur, pl.ds(r0, nb), 0:mpb]
            bi = bu_scr[cur, pl.ds(r0, nb), mpb:2 * mpb]
            xr, xi = lr * xr - li * xi + br, lr * xi + li * xr + bi
            bu_scr[cur, pl.ds(r0, nb), 0:mpb] = xr
            bu_scr[cur, pl.ds(r0, nb), mpb:2 * mpb] = xi
        st_scr[j, :, 0:mpb] = xr
        st_scr[j, :, mpb:2 * mpb] = xi

    if emit_y:
        readout(nj - 1)
        for cp in out_copies(g, slot):
            cp.start()

        @pl.when(g == ng - 1)
        def _():
            for cp in out_copies(g - 1, 1 - slot):
                cp.wait()
            for cp in out_copies(g, slot):
                cp.wait()

    @pl.when(t == nt - 1)
    def _():
        fin_ref[0] = st_scr[...]


def _s5_scan(u_arr, bw, cw, lamr, lami, init, tl, emit_y):
    b, l = u_arr.shape[:2]
    assert b == SUBLANES and l % tl == 0
    nj = bw.shape[1]
    c = nj * LANES
    nt = l // tl
    m2 = 2 * MODES_PER_BLOCK
    wmap = lambda g: (g // nt, 0, 0, 0)
    in_specs = [pl.BlockSpec(memory_space=pl.ANY), pl.BlockSpec((1, nj, LANES, m2), wmap)]
    args = [u_arr, bw]
    if emit_y:
        in_specs.append(pl.BlockSpec((1, nj, m2, LANES), wmap))
        args.append(cw)
    in_specs += [pl.BlockSpec((1, nj, 1, MODES_PER_BLOCK), wmap),
                 pl.BlockSpec((1, nj, 1, MODES_PER_BLOCK), wmap),
                 pl.BlockSpec((1, nj, SUBLANES, m2), wmap)]
    args += [lamr, lami, init]
    fin_spec = pl.BlockSpec((1, nj, SUBLANES, m2), wmap)
    fin_shape = jax.ShapeDtypeStruct((2, nj, SUBLANES, m2), F32)
    scratch = [pltpu.VMEM((2, tl, SUBLANES, c), F32),
               pltpu.VMEM((3, tl * SUBLANES, m2), F32),
               pltpu.VMEM((nj, SUBLANES, m2), F32),
               pltpu.SemaphoreType.DMA((2,))]
    if emit_y:
        out_specs = [pl.BlockSpec(memory_space=pl.ANY), fin_spec]
        out_shape = [jax.ShapeDtypeStruct((2, b, l, c), F32), fin_shape]
        scratch += [pltpu.VMEM((2, tl, SUBLANES, c), F32), pltpu.SemaphoreType.DMA((2,))]
    else:
        out_specs = [fin_spec]
        out_shape = [fin_shape]
    return pl.pallas_call(
        functools.partial(_s5_body, tl=tl, nt=nt, nj=nj, emit_y=emit_y),
        grid=(2 * nt,),
        in_specs=in_specs, out_specs=out_specs, out_shape=out_shape,
        scratch_shapes=scratch,
        compiler_params=_cparams(("arbitrary",), 48),
        name="s5scan_lat" if emit_y else "s5scan_ctx",
    )(*args)


def _hyfilt_body(z_ref, w1_ref, b1_ref, w2_ref, b2_ref, w3_ref, b3_ref, fr_ref,
                 w4f_ref, w4b_ref, df_ref, db_ref, ka_ref, kd_ref, h_scr, *, length):
    z = z_ref[...]

    @pl.when(pl.program_id(0) == 0)
    def _():
        fr = fr_ref[...]
        h = jnp.sin(fr * (_dot_hi(z, w1_ref[...]) + b1_ref[...]))
        h = jnp.sin(fr * (_dot_hi(h, w2_ref[...]) + b2_ref[...]))
        h_scr[...] = jnp.sin(fr * (_dot_hi(h, w3_ref[...]) + b3_ref[...]))

    h = h_scr[...]
    t = z[:, 0:1]
    hf = _dot_hi(h, w4f_ref[...]) * jnp.exp(-t * jnp.abs(df_ref[...]))
    hb = _dot_hi(h, w4b_ref[...]) * jnp.exp(-t * jnp.abs(db_ref[...]))
    row = lax.broadcasted_iota(I32, hf.shape, 0)
    hb = jnp.where(row < length - 1, hb, 0.0)
    norm = jnp.sum(jnp.abs(hf), axis=0, keepdims=True) + jnp.sum(jnp.abs(hb), axis=0, keepdims=True)
    kf = hf / norm
    kb = jnp.where(row >= 1, pltpu.roll(hb, 1, axis=0), 0.0) / norm
    ka_ref[...] = kf + kb
    kd_ref[...] = kf - kb


def _hyena_filter(z, w1, b1, w2, b2, w3, b3, fr, w4, deltas, ct=256):
    length, ze = z.shape
    order = w2.shape[0]
    c = w4.shape[1] // 2
    nct = c // ct
    full = lambda shape: pl.BlockSpec(shape, lambda i: (0, 0))
    shp = jax.ShapeDtypeStruct((length, c), F32)
    return pl.pallas_call(
        functools.partial(_hyfilt_body, length=length),
        grid=(nct,),
        in_specs=[full((length, ze)), full((ze, order)), full((1, order)),
                  full((order, order)), full((1, order)), full((order, order)), full((1, order)),
                  full((1, order)),
                  pl.BlockSpec((order, ct), lambda i: (0, i)),
                  pl.BlockSpec((order, ct), lambda i: (0, nct + i)),
                  pl.BlockSpec((1, ct), lambda i: (0, i)),
                  pl.BlockSpec((1, ct), lambda i: (0, nct + i))],
        out_specs=[pl.BlockSpec((length, ct), lambda i: (0, i))] * 2,
        out_shape=[shp, shp],
        scratch_shapes=[pltpu.VMEM((length, order), F32)],
        compiler_params=_cparams(("arbitrary",), 48),
        name="hyena_filter",
    )(z, w1, b1, w2, b2, w3, b3, fr, w4, w4, deltas, deltas)


def _dfttab_body(ca_ref, sa_ref, cb_ref, sb_ref, c_ref, s_ref):
    ca = ca_ref[0]
    sa = sa_ref[0]
    cb = cb_ref[...]
    sb = sb_ref[...]
    c_ref[...] = (ca * cb - sa * sb).astype(c_ref.dtype)
    s_ref[...] = (sa * cb + ca * sb).astype(s_ref.dtype)


def _dft_tables(length, r=128):
    n = 2 * length
    s = jnp.arange(length, dtype=I32)[None, :]
    ang = lambda k: (2.0 * math.pi / n) * ((k * s) % n).astype(F32)
    aa = ang(jnp.arange(length // r, dtype=I32)[:, None] * r)
    ab = ang(jnp.arange(r, dtype=I32)[:, None])
    rows3 = lambda a: a.reshape(length // r, 1, length)
    shp = jax.ShapeDtypeStruct((length, length), BF16)
    return pl.pallas_call(
        _dfttab_body,
        grid=(length // r,),
        in_specs=[pl.BlockSpec((None, 1, length), lambda i: (i, 0, 0)),
                  pl.BlockSpec((None, 1, length), lambda i: (i, 0, 0)),
                  pl.BlockSpec((r, length), lambda i: (0, 0)),
                  pl.BlockSpec((r, length), lambda i: (0, 0))],
        out_specs=[pl.BlockSpec((r, length), lambda i: (i, 0))] * 2,
        out_shape=[shp, shp],
        compiler_params=_cparams(("arbitrary",), 32),
        name="dft_tables",
    )(rows3(jnp.cos(aa)), rows3(jnp.sin(aa)), jnp.cos(ab), jnp.sin(ab))


def _hyspec_body(c_ref, s_ref, ka_ref, kd_ref, kk_ref, kn_ref, *, length, fq):
    q = pl.program_id(1)
    nq = length // fq
    n = 2.0 * length

    @pl.when(q < nq)
    def _():
        grow = q * fq + lax.broadcasted_iota(I32, (fq, 1), 0)
        scale = jnp.where(grow == 0, 1.0 / n, 2.0 / n)
        kk_ref[...] = _dot(c_ref[...], ka_ref[...].astype(BF16)) * scale

    @pl.when(q >= nq)
    def _():
        kk_ref[...] = _dot(s_ref[...], kd_ref[...].astype(BF16)) * (2.0 / n)

    @pl.when(q == 0)
    def _():
        a = ka_ref[...]
        row = lax.broadcasted_iota(I32, a.shape, 0)
        sgn = jnp.where(row % 2 == 0, 1.0, -1.0)
        kn = jnp.sum(a * sgn, axis=0, keepdims=True) * (1.0 / n)
        kn_ref[...] = jnp.broadcast_to(kn, kn_ref.shape)


def _hyena_spectrum(ctab, stab, ka, kd, ct=256, fq=512):
    length, c = ka.shape
    nq = length // fq
    return pl.pallas_call(
        functools.partial(_hyspec_body, length=length, fq=fq),
        grid=(c // ct, 2 * nq),
        in_specs=[pl.BlockSpec((fq, length), lambda i, q: (jnp.minimum(q, nq - 1), 0)),
                  pl.BlockSpec((fq, length), lambda i, q: (jnp.maximum(q - nq, 0), 0)),
                  pl.BlockSpec((length, ct), lambda i, q: (0, i)),
                  pl.BlockSpec((length, ct), lambda i, q: (0, i))],
        out_specs=[pl.BlockSpec((fq, ct), lambda i, q: (q, i)),
                   pl.BlockSpec((SUBLANES, ct), lambda i, q: (0, i))],
        out_shape=[jax.ShapeDtypeStruct((2 * length, c), F32), jax.ShapeDtypeStruct((SUBLANES, c), F32)],
        compiler_params=_cparams(("arbitrary", "arbitrary"), 48),
        name="hyena_spectrum",
    )(ctab, stab, ka, kd)


def _inproj_hy_body(x_ref, g_ref, sh_ref, sc_ref, wv_ref, w1_ref, w0_ref, cv_ref, c1_ref, c0_ref,
                    bv_ref, b1_ref, b0_ref, v_ref, x0_ref):
    half = x_ref.shape[1] // 2
    row = lax.broadcasted_iota(I32, (half, 1), 0) % GRID_W

    def sconv(a, w_ref, b_ref):
        prev = jnp.where(row == 0, 0.0, pltpu.roll(a, 1, axis=0))
        nxt = jnp.where(row == GRID_W - 1, 0.0, pltpu.roll(a, half - 1, axis=0))
        return prev * w_ref[0:1, :] + a * w_ref[1:2, :] + nxt * w_ref[2:3, :] + b_ref[...]

    for r0 in (0, half):
        rs = slice(r0, r0 + half)
        h = _norm_mod(x_ref[0, rs, :], g_ref[...], sh_ref[0], sc_ref[0]).astype(BF16)
        zv = sconv(_dot(h, wv_ref[...]), cv_ref, bv_ref)
        z1 = sconv(_dot(h, w1_ref[...]), c1_ref, b1_ref)
        v_ref[0, rs, :] = (zv * z1).astype(v_ref.dtype)
        x0_ref[0, rs, :] = sconv(_dot(h, w0_ref[...]), c0_ref, b0_ref).astype(x0_ref.dtype)


def _inproj_hyena(x, g, sh, sc, w, conv_w, conv_b, c, first_col, tm=512, ct=256):
    b, l, d = x.shape
    assert tm % (2 * GRID_W) == 0
    nct = c // ct
    base = first_col // ct
    tokmap = lambda ci, bi, li: (bi, li, 0)
    perb = pl.BlockSpec((1, 1, d), lambda ci, bi, li: (bi, 0, 0))
    wspec = lambda k: pl.BlockSpec((d, ct), lambda ci, bi, li: (0, base + k * nct + ci))
    cspec = lambda k: pl.BlockSpec((3, ct), lambda ci, bi, li: (0, k * nct + ci))
    bspec = lambda k: pl.BlockSpec((1, ct), lambda ci, bi, li: (0, k * nct + ci))
    shp = jax.ShapeDtypeStruct((b, l, c), BF16)
    return pl.pallas_call(
        _inproj_hy_body,
        grid=(nct, b, l // tm),
        in_specs=[pl.BlockSpec((1, tm, d), tokmap), pl.BlockSpec((1, d), lambda ci, bi, li: (0, 0)), perb, perb,
                  wspec(0), wspec(1), wspec(2), cspec(0), cspec(1), cspec(2), bspec(0), bspec(1), bspec(2)],
        out_specs=[pl.BlockSpec((1, tm, ct), lambda ci, bi, li: (bi, li, ci))] * 2,
        out_shape=[shp, shp],
        compiler_params=_cparams(("arbitrary", "arbitrary", "arbitrary"), 40),
        name="inproj_hyena",
    )(x, g, sh, sc, w, w, w, conv_w, conv_w, conv_w, conv_b, conv_b, conv_b)


def _hyconv_body(v_ref, x0_ref, kk_ref, kn_ref, d_ref, c_hbm, s_hbm, o_ref, c_scr, s_scr, sem, *, length, fq):
    first = jnp.logical_and(pl.program_id(0) == 0, pl.program_id(1) == 0)

    @pl.when(first)
    def _():
        cc = pltpu.make_async_copy(c_hbm, c_scr, sem.at[0])
        cs = pltpu.make_async_copy(s_hbm, s_scr, sem.at[1])
        cc.start()
        cs.start()
        cc.wait()
        cs.wait()

    vb = v_ref[0]
    v = vb.astype(F32)
    row = lax.broadcasted_iota(I32, (length, 1), 0)
    sgn = jnp.where(row % 2 == 0, 1.0, -1.0)
    vn = jnp.sum(v * sgn, axis=0, keepdims=True)
    acc = sgn * (vn * kn_ref[0:1, :])
    for q in range(length // fq):
        lo, hi = q * fq, (q + 1) * fq
        vr = _dot(c_scr[lo:hi, :], vb)
        vs = _dot(s_scr[lo:hi, :], vb)
        kr = kk_ref[lo:hi, :]
        ks = kk_ref[length + lo:length + hi, :]
        pr = vr * kr - vs * ks
        ps = vr * ks + vs * kr
        acc = acc + _dot(c_scr[:, lo:hi], pr.astype(BF16)) + _dot(s_scr[:, lo:hi], ps.astype(BF16))
    o_ref[0] = ((acc + v * d_ref[...]) * x0_ref[0].astype(F32)).astype(o_ref.dtype)


def _hyena_conv(v, x0, kk, kn, d_skip, ctab, stab, ct=256, fq=512):
    b, l, c = v.shape
    return pl.pallas_call(
        functools.partial(_hyconv_body, length=l, fq=fq),
        grid=(c // ct, b),
        in_specs=[pl.BlockSpec((1, l, ct), lambda ci, bi: (bi, 0, ci)),
                  pl.BlockSpec((1, l, ct), lambda ci, bi: (bi, 0, ci)),
                  pl.BlockSpec((2 * l, ct), lambda ci, bi: (0, ci)),
                  pl.BlockSpec((SUBLANES, ct), lambda ci, bi: (0, ci)),
                  pl.BlockSpec((1, ct), lambda ci, bi: (0, ci)),
                  pl.BlockSpec(memory_space=pl.ANY),
                  pl.BlockSpec(memory_space=pl.ANY)],
        out_specs=pl.BlockSpec((1, l, ct), lambda ci, bi: (bi, 0, ci)),
        out_shape=jax.ShapeDtypeStruct((b, l, c), BF16),
        scratch_shapes=[pltpu.VMEM((l, l), BF16), pltpu.VMEM((l, l), BF16),
                        pltpu.SemaphoreType.DMA((2,))],
        compiler_params=_cparams(("arbitrary", "arbitrary"), 48),
        name="hyena_conv",
    )(v, x0, kk, kn, d_skip, ctab, stab)


def _split_bf16(a):
    hi = a.astype(BF16)
    return hi, (a - hi.astype(F32)).astype(BF16)


def _outproj_body(y_ref, u_ref, yh_ref, x_ref, d_ref, wg_ref, bg_ref, wo_ref, ga_ref, g2_ref,
                  sh_ref, sc_ref, wrh_ref, wrl_ref, br_ref, x1_ref, hp_ref, te_ref, gt_ref, cnt_ref, *, sub):
    tm, c = u_ref.shape[1], u_ref.shape[2]

    @pl.when(jnp.logical_and(pl.program_id(0) == 0, pl.program_id(1) == 0))
    def _():
        cnt_ref[...] = jnp.zeros_like(cnt_ref)

    lane = lax.broadcasted_iota(I32, (sub, LANES), 1)
    total = jnp.zeros((1, LANES), F32)
    for r0 in range(0, tm, sub):
        rs = slice(r0, r0 + sub)
        ys = y_ref[0, 0, rs, :] + y_ref[1, 0, rs, :] + u_ref[0, rs, :] * d_ref[...]
        ge = 0.5 * ys * (1.0 + lax.erf(ys * (1.0 / math.sqrt(2.0))))
        gl = ge * jax.nn.sigmoid(_dot(ge.astype(BF16), wg_ref[...]) + bg_ref[...])
        mix = _dot(gl.astype(BF16), wo_ref[0:c, :]) + _dot(yh_ref[0, rs, :], wo_ref[c:2 * c, :])
        x1 = x_ref[0, rs, :] + ga_ref[0] * mix
        x1_ref[0, rs, :] = x1
        h2 = _norm_mod(x1, g2_ref[...], sh_ref[0], sc_ref[0])
        hp_ref[0, rs, :] = h2

        hh, hl = _split_bf16(h2)
        logits = _dot(hh, wrh_ref[...]) + (_dot(hh, wrl_ref[...]) + _dot(hl, wrh_ref[...])) + br_ref[...]
        work = logits
        vals, idxs = [], []
        for _ in range(TOP_K):
            m = jnp.max(work, axis=-1, keepdims=True)
            ix = jnp.min(jnp.where(work == m, lane, LANES), axis=-1, keepdims=True)
            vals.append(m)
            idxs.append(ix)
            work = jnp.where(lane == ix, NEG_BIG, work)
        ex = [jnp.exp(v - vals[0]) for v in vals]
        den = ex[0]
        for e in ex[1:]:
            den = den + e
        te = jnp.zeros((sub, LANES), I32)
        gt = jnp.zeros((sub, LANES), F32)
        mh = jnp.zeros((sub, LANES), F32)
        for k in range(TOP_K):
            te = jnp.where(lane == k, idxs[k], te)
            gt = jnp.where(lane == k, ex[k] / den, gt)
            mh = mh + (lane == idxs[k]).astype(F32)
        te_ref[0, rs, :] = te
        gt_ref[0, rs, :] = gt
        total = total + jnp.sum(mh, axis=0, keepdims=True)

    cnt_ref[...] += jnp.broadcast_to(total, cnt_ref.shape)


def _outproj(y, p, yh, x, d_skip, w_glu, b_glu, w_out, ga1, g2, sh2, sc2, w_rh, w_rl, b_r, tm=256, sub=128):
    b, l, dm = x.shape
    c = yh.shape[2]
    tok = lambda width: pl.BlockSpec((1, tm, width), lambda bi, li: (bi, li, 0))
    full = lambda shape: pl.BlockSpec(shape, lambda bi, li: (0,) * len(shape))
    perb = pl.BlockSpec((1, 1, dm), lambda bi, li: (bi, 0, 0))
    return pl.pallas_call(
        functools.partial(_outproj_body, sub=sub),
        grid=(b, l // tm),
        in_specs=[pl.BlockSpec((2, 1, tm, c), lambda bi, li: (0, bi, li, 0)),
                  tok(c), tok(c), tok(dm),
                  full((1, c)), full((c, c)), full((1, c)), full((2 * c, dm)),
                  perb, full((1, dm)), perb, perb,
                  full((dm, LANES)), full((dm, LANES)), full((1, LANES))],
        out_specs=[tok(dm), tok(dm), tok(LANES), tok(LANES), full((SUBLANES, LANES))],
        out_shape=[jax.ShapeDtypeStruct((b, l, dm), F32),
                   jax.ShapeDtypeStruct((b, l, dm), F32),
                   jax.ShapeDtypeStruct((b, l, LANES), I32),
                   jax.ShapeDtypeStruct((b, l, LANES), F32),
                   jax.ShapeDtypeStruct((SUBLANES, LANES), F32)],
        compiler_params=_cparams(("arbitrary", "arbitrary"), 56),
        name="outproj_router",
    )(y, p, yh, x, d_skip, w_glu, b_glu, w_out, ga1, g2, sh2, sc2, w_rh, w_rl, b_r)


def _slots_body(te_ref, seg_ref, pos_ref, carry):
    @pl.when(pl.program_id(0) == 0)
    def _():
        carry[...] = jnp.zeros_like(carry)

    te = te_ref[...]
    tm = te.shape[0]
    lane = lax.broadcasted_iota(I32, (tm, LANES), 1)
    sel = [lane == te[:, k:k + 1] for k in range(TOP_K)]
    mh = jnp.zeros((tm, LANES), F32)
    for s in sel:
        mh = mh + s.astype(F32)
    r = lax.broadcasted_iota(I32, (tm, tm), 0)
    cc = lax.broadcasted_iota(I32, (tm, tm), 1)
    strict_lower = (r > cc).astype(BF16)
    slot = _dot(strict_lower, mh.astype(BF16)) + carry[0:1, :] + seg_ref[0:1, :]
    pos = jnp.zeros((tm, LANES), F32)
    for k in range(TOP_K):
        pk = jnp.sum(jnp.where(sel[k], slot, 0.0), axis=-1, keepdims=True)
        pos = jnp.where(lane == k, pk, pos)
    pos_ref[...] = pos.astype(I32)
    carry[...] += jnp.broadcast_to(jnp.sum(mh, axis=0, keepdims=True), carry.shape)


def _slots(te, seg_start, tm=512):
    t = te.shape[0]
    return pl.pallas_call(
        _slots_body,
        grid=(t // tm,),
        in_specs=[pl.BlockSpec((tm, LANES), lambda i: (i, 0)),
                  pl.BlockSpec((SUBLANES, LANES), lambda i: (0, 0))],
        out_specs=pl.BlockSpec((tm, LANES), lambda i: (i, 0)),
        out_shape=jax.ShapeDtypeStruct((t, LANES), I32),
        scratch_shapes=[pltpu.VMEM((SUBLANES, LANES), F32)],
        compiler_params=_cparams(("arbitrary",), 32),
        name="moe_slots",
    )(te, seg_start)


PAD_CHUNKS = tuple(s for s in (ROW_BLOCK >> k for k in range(1, ROW_BLOCK.bit_length())) if s >= SUBLANES)


def _row_copy(src, src_row, dst, dst_row, sem):
    return pltpu.make_async_copy(src.at[pl.ds(src_row, 1)], dst.at[pl.ds(dst_row, 1)], sem)


def _dispatch_body(pos_ref, ps_ref, pn_ref, h_ref, xs_out, zbuf, sem, zsem, *, tm, n_exp, n_blocks):
    @pl.when(pl.program_id(0) == 0)
    def _():
        zbuf[...] = jnp.zeros_like(zbuf)

        def copies(e):
            start = ps_ref[e]
            head = (-start) & (SUBLANES - 1)
            out = [(r < head, _row_copy(zbuf, 0, xs_out, start + r, zsem)) for r in range(SUBLANES - 1)]
            body = pn_ref[e] - head
            for size in PAD_CHUNKS:
                done = body & (-2 * size)
                dst = xs_out.at[pl.ds(pl.multiple_of(start + head + done, SUBLANES), size)]
                out.append(((body & size) != 0, pltpu.make_async_copy(zbuf.at[pl.ds(0, size)], dst, zsem)))
            return out

        def fill(e, carry):
            for on, cp in copies(e):
                pl.when(on)(cp.start)
            return carry

        def drain(e, carry):
            for on, cp in copies(e):
                pl.when(on)(cp.wait)
            return carry

        lax.fori_loop(0, n_exp, fill, 0)
        lax.fori_loop(0, n_exp, drain, 0)

        def tail(blk):
            rows = [pl.multiple_of(blk * ROW_BLOCK + part * PAD_CHUNKS[0], SUBLANES)
                    for part in range(ROW_BLOCK // PAD_CHUNKS[0])]
            return [pltpu.make_async_copy(zbuf, xs_out.at[pl.ds(r, PAD_CHUNKS[0])], zsem) for r in rows]

        def fill_tail(blk, carry):
            for cp in tail(blk):
                cp.start()
            return carry

        def drain_tail(blk, carry):
            for cp in tail(blk):
                cp.wait()
            return carry

        first_unused = (ps_ref[n_exp - 1] + pn_ref[n_exp - 1]) // ROW_BLOCK
        lax.fori_loop(first_unused, n_blocks, fill_tail, 0)
        lax.fori_loop(first_unused, n_blocks, drain_tail, 0)

    base = pl.program_id(0) * (tm * TOP_K)

    def issue(r, carry):
        for k in range(TOP_K):
            _row_copy(h_ref, r, xs_out, pos_ref[base + r * TOP_K + k], sem).start()
        return carry

    lax.fori_loop(0, tm, issue, 0, unroll=8)

    def drain_rows(r, carry):
        for k in range(TOP_K):
            _row_copy(h_ref, 0, xs_out, 0, sem).wait()
        return carry

    lax.fori_loop(0, tm, drain_rows, 0, unroll=8)


def _dispatch(pos_flat, pad_start, pad_len, hp, n_rows, tm=256):
    t, w = hp.shape
    return pl.pallas_call(
        functools.partial(_dispatch_body, tm=tm, n_exp=pad_start.shape[0], n_blocks=n_rows // ROW_BLOCK),
        grid_spec=pltpu.PrefetchScalarGridSpec(
            num_scalar_prefetch=3,
            grid=(t // tm,),
            in_specs=[pl.BlockSpec((tm, w), lambda i, pos, ps, pn: (i, 0))],
            out_specs=pl.BlockSpec(memory_space=pl.ANY),
            scratch_shapes=[pltpu.VMEM((PAD_CHUNKS[0], w), hp.dtype),
                            pltpu.SemaphoreType.DMA(()), pltpu.SemaphoreType.DMA(())]),
        out_shape=jax.ShapeDtypeStruct((n_rows, w), hp.dtype),
        compiler_params=_cparams(("arbitrary",), 32),
        name="moe_dispatch",
    )(pos_flat, pad_start, pad_len, hp)


class _CopyGroup:
    def __init__(self, copies):
        self.copies = copies

    def start(self):
        for cp in self.copies:
            cp.start()

    def wait(self):
        for cp in self.copies:
            cp.wait()


def _stream_expert_blocks(first, count, total, n_blocks, src_hbm, dst_hbm, col0, ibuf, obuf, isem, osem, compute):
    e = pl.program_id(1)
    rb = ROW_BLOCK
    n_in = ibuf.shape[0]
    width = obuf.shape[2]

    part = rb // DMA_SPLIT

    def rows(gs, k):
        return pl.ds(pl.multiple_of(gs * rb + k * part, part), part)

    def in_copy(gs):
        slot = gs % n_in
        return _CopyGroup([pltpu.make_async_copy(src_hbm.at[rows(gs, k)], ibuf.at[slot, pl.ds(k * part, part)],
                                                 isem.at[slot]) for k in range(DMA_SPLIT)])

    def out_copy(gs, slot):
        return _CopyGroup([pltpu.make_async_copy(obuf.at[slot, pl.ds(k * part, part)],
                                                 dst_hbm.at[rows(gs, k), pl.ds(col0, width)], osem.at[slot])
                           for k in range(DMA_SPLIT)])

    @pl.when(e == 0)
    def _():
        for gs in range(n_in - 1):
            pl.when(gs < total)(in_copy(gs).start)

    def block(s, carry):
        gs = first + s
        slot = gs % 2

        @pl.when(gs + n_in - 1 < total)
        def _():
            in_copy(gs + n_in - 1).start()

        in_copy(gs).wait()

        @pl.when(gs >= 2)
        def _():
            out_copy(gs - 2, slot).wait()

        compute(ibuf.at[gs % n_in], obuf.at[slot])
        out_copy(gs, slot).start()
        return carry

    lax.fori_loop(0, count, block, 0)

    @pl.when(e == pl.num_programs(1) - 1)
    def _():
        @pl.when(total >= 2)
        def _():
            out_copy(total - 2, total % 2).wait()

        out_copy(total - 1, (total - 1) % 2).wait()
        obuf[0] = jnp.zeros(obuf.shape[1:], obuf.dtype)

        def fill(gs, carry):
            out_copy(gs, 0).start()
            return carry

        def drain(gs, carry):
            out_copy(gs, 0).wait()
            return carry

        lax.fori_loop(total, n_blocks, fill, 0)
        lax.fori_loop(total, n_blocks, drain, 0)


def _gateup_body(fs_ref, ns_ref, tot_ref, xs_hbm, wg_ref, wu_ref, bg_ref, bu_ref, h_hbm,
                 wg_s, wu_s, xbuf, obuf, xsem, osem, *, n_blocks):
    e = pl.program_id(1)
    tn = obuf.shape[2]
    half = tn // 2

    @pl.when(ns_ref[e] > 0)
    def _():
        wg_s[...] = wg_ref[...].astype(BF16)
        wu_s[...] = wu_ref[...].astype(BF16)

    def compute(x_ref, o_ref):
        x = x_ref[...].astype(BF16)
        for c0 in (0, half):
            cs = slice(c0, c0 + half)
            g = _dot(x, wg_s[:, cs]) + bg_ref[:, cs]
            u = _dot(x, wu_s[:, cs]) + bu_ref[:, cs]
            g = jnp.minimum(g, SWIGLU_LIMIT)
            u = jnp.clip(u, -SWIGLU_LIMIT, SWIGLU_LIMIT)
            o_ref[:, cs] = ((u + 1.0) * (g * jax.nn.sigmoid(SWIGLU_ALPHA * g))).astype(o_ref.dtype)

    col0 = pl.multiple_of(pl.program_id(0) * tn, tn)
    _stream_expert_blocks(fs_ref[e], ns_ref[e], tot_ref[0], n_blocks, xs_hbm, h_hbm, col0,
                          xbuf, obuf, xsem, osem, compute)


def _expert_gateup(first_blk, n_blk, total, xs, w_gate_up, b_gate_up, tn=1024):
    n_exp, dm, ff2 = w_gate_up.shape
    ff = ff2 // 2
    nr = xs.shape[0]
    nn = ff // tn
    return pl.pallas_call(
        functools.partial(_gateup_body, n_blocks=nr // ROW_BLOCK),
        grid_spec=pltpu.PrefetchScalarGridSpec(
            num_scalar_prefetch=3,
            grid=(nn, n_exp),
            in_specs=[pl.BlockSpec(memory_space=pl.ANY),
                      pl.BlockSpec((None, dm, tn), lambda j, e, fs, ns, tot: (e, 0, j)),
                      pl.BlockSpec((None, dm, tn), lambda j, e, fs, ns, tot: (e, 0, nn + j)),
                      pl.BlockSpec((None, 1, tn), lambda j, e, fs, ns, tot: (e, 0, j)),
                      pl.BlockSpec((None, 1, tn), lambda j, e, fs, ns, tot: (e, 0, nn + j))],
            out_specs=pl.BlockSpec(memory_space=pl.ANY),
            scratch_shapes=[pltpu.VMEM((dm, tn), BF16), pltpu.VMEM((dm, tn), BF16),
                            pltpu.VMEM((IN_SLOTS, ROW_BLOCK, dm), xs.dtype), pltpu.VMEM((2, ROW_BLOCK, tn), BF16),
                            pltpu.SemaphoreType.DMA((IN_SLOTS,)), pltpu.SemaphoreType.DMA((2,))]),
        out_shape=jax.ShapeDtypeStruct((nr, ff), BF16),
        compiler_params=_cparams(("arbitrary", "arbitrary"), 56),
        name="moe_gateup",
    )(first_blk, n_blk, total, xs, w_gate_up, w_gate_up, b_gate_up, b_gate_up)


def _down_body(fs_ref, ns_ref, tot_ref, h_hbm, w_ref, b_ref, y_hbm, w_s, hbuf, obuf, hsem, osem, *, n_blocks):
    e = pl.program_id(1)
    tn = obuf.shape[2]

    @pl.when(ns_ref[e] > 0)
    def _():
        w_s[...] = w_ref[...].astype(BF16)

    def compute(h_ref, o_ref):
        o_ref[...] = _dot(h_ref[...], w_s[...]) + b_ref[...]

    col0 = pl.multiple_of(pl.program_id(0) * tn, tn)
    _stream_expert_blocks(fs_ref[e], ns_ref[e], tot_ref[0], n_blocks, h_hbm, y_hbm, col0,
                          hbuf, obuf, hsem, osem, compute)


def _expert_down(first_blk, n_blk, total, h, w_down, b_down, tn=2048):
    n_exp, ff, dm = w_down.shape
    nr = h.shape[0]
    return pl.pallas_call(
        functools.partial(_down_body, n_blocks=nr // ROW_BLOCK),
        grid_spec=pltpu.PrefetchScalarGridSpec(
            num_scalar_prefetch=3,
            grid=(dm // tn, n_exp),
            in_specs=[pl.BlockSpec(memory_space=pl.ANY),
                      pl.BlockSpec((None, ff, tn), lambda j, e, fs, ns, tot: (e, 0, j)),
                      pl.BlockSpec((None, 1, tn), lambda j, e, fs, ns, tot: (e, 0, j))],
            out_specs=pl.BlockSpec(memory_space=pl.ANY),
            scratch_shapes=[pltpu.VMEM((ff, tn), BF16),
                            pltpu.VMEM((IN_SLOTS, ROW_BLOCK, ff), h.dtype), pltpu.VMEM((2, ROW_BLOCK, tn), F32),
                            pltpu.SemaphoreType.DMA((IN_SLOTS,)), pltpu.SemaphoreType.DMA((2,))]),
        out_shape=jax.ShapeDtypeStruct((nr, dm), F32),
        compiler_params=_cparams(("arbitrary", "arbitrary"), 56),
        name="moe_down",
    )(first_blk, n_blk, total, h, w_down, b_down)


def _combine_body(pos_ref, x1_ref, gt_ref, ga_ref, gf_ref, y_hbm, o_ref, ybuf, sem, *, tm):
    i = pl.program_id(0)
    n = pl.num_programs(0)

    def gather(tile, slot):
        def issue(r, carry):
            for k in range(TOP_K):
                p = pos_ref[(tile * tm + r) * TOP_K + k]
                _row_copy(y_hbm, p, ybuf.at[slot, k], r, sem.at[slot]).start()
            return carry
        lax.fori_loop(0, tm, issue, 0, unroll=8)

    @pl.when(i == 0)
    def _():
        gather(0, 0)

    @pl.when(i + 1 < n)
    def _():
        gather(i + 1, (i + 1) % 2)

    slot = i % 2

    def drain(r, carry):
        for k in range(TOP_K):
            _row_copy(y_hbm, 0, ybuf.at[slot, k], 0, sem.at[slot]).wait()
        return carry

    lax.fori_loop(0, tm, drain, 0, unroll=8)

    gt = gt_ref[0]
    acc = gt[:, 0:1] * ybuf[slot, 0]
    for k in range(1, TOP_K):
        acc = acc + gt[:, k:k + 1] * ybuf[slot, k]
    x2 = x1_ref[0] + ga_ref[0] * acc
    ms = jnp.mean(x2 * x2, axis=-1, keepdims=True)
    o_ref[0] = x2 * lax.rsqrt(ms + EPS) * gf_ref[...]


def _combine(pos_flat, x1, gate, ga2, g_final, y, tm=128):
    b, l, dm = x1.shape
    per = l // tm
    tok = lambda width: pl.BlockSpec((1, tm, width), lambda i, pos: (i // per, i % per, 0))
    return pl.pallas_call(
        functools.partial(_combine_body, tm=tm),
        grid_spec=pltpu.PrefetchScalarGridSpec(
            num_scalar_prefetch=1,
            grid=(b * per,),
            in_specs=[tok(dm), tok(LANES),
                      pl.BlockSpec((1, 1, dm), lambda i, pos: (i // per, 0, 0)),
                      pl.BlockSpec((1, dm), lambda i, pos: (0, 0)),
                      pl.BlockSpec(memory_space=pl.ANY)],
            out_specs=tok(dm),
            scratch_shapes=[pltpu.VMEM((2, TOP_K, tm, dm), F32), pltpu.SemaphoreType.DMA((2,))]),
        out_shape=jax.ShapeDtypeStruct((b, l, dm), F32),
        compiler_params=_cparams(("arbitrary",), 48),
        name="moe_combine",
    )(pos_flat, x1, gate, ga2, g_final, y)


def _filter_features(length):
    t = jnp.linspace(0.0, 1.0, length, dtype=F32)[:, None]
    bands = jnp.linspace(1e-4, HY_BANDS - 1, HY_BANDS, dtype=F32)
    ang = (2 * math.pi / length) * jnp.arange(length, dtype=F32)[:, None] * bands
    z = jnp.concatenate([t, jnp.cos(ang), -jnp.sin(ang)], axis=-1)
    return jnp.pad(z, ((0, 0), (0, LANES - z.shape[1])))


def _segments(counts):
    padded = (counts + ROW_BLOCK - 1) // ROW_BLOCK * ROW_BLOCK
    seg_end = jnp.cumsum(padded)
    seg_start = seg_end - padded
    total = (seg_end[-1:] // ROW_BLOCK).astype(I32)
    return (seg_start, (seg_start // ROW_BLOCK).astype(I32), (padded // ROW_BLOCK).astype(I32), total,
            seg_start + counts, padded - counts)


def kernel(x, c, ctx, c_ctx, w_ada, b_ada, g_norm1, g_norm2, w_in, s5_lam_re, s5_lam_im, s5_log_step,
           s5_b_re, s5_b_im, s5_c_re, s5_c_im, s5_d, s5_w_glu, s5_b_glu, hy_conv_w, hy_conv_b,
           hy_w1, hy_b1, hy_w2, hy_b2, hy_w3, hy_b3, hy_w4, hy_freq, hy_deltas, hy_d, w_out,
           w_router, b_router, w_gate_up, b_gate_up, w_down, b_down, g_final):
    assert w_ada.shape[0] == 1, "single-layer block"
    b, l, dm = x.shape
    lc = ctx.shape[1]
    s5w = s5_d.shape[1]
    hyw = hy_d.shape[1]
    n_exp = w_router.shape[2]
    row = lambda a: a.reshape(1, -1)

    cc = jnp.concatenate([c, c_ctx[None], jnp.zeros((2 * SUBLANES - b - 1, dm), F32)], axis=0)
    mod = _adaln(cc, w_ada[0], row(b_ada[0]))
    sh1, sc1, ga1, sh2, sc2, ga2 = [mod[:b, k * dm:(k + 1) * dm].reshape(b, 1, dm) for k in range(N_MOD)]
    csh1, csc1 = [mod[b:b + 1, k * dm:(k + 1) * dm].reshape(1, 1, dm) for k in range(2)]

    w_in_b = w_in[0].astype(BF16)
    g1 = row(g_norm1[0])
    p = _inproj(x, g1, sh1, sc1, w_in_b, s5w, tm=512)
    p_ctx = _inproj(ctx, g1, csh1, csc1, w_in_b, s5w, tm=lc)
    v, x0 = _inproj_hyena(x, g1, sh1, sc1, w_in_b, hy_conv_w[0], row(hy_conv_b[0]), hyw, s5w)

    bw, cw, lamr, lami = _s5_weights(s5_lam_re[0], s5_lam_im[0], s5_log_step[0],
                                     s5_b_re[0], s5_b_im[0], s5_c_re[0], s5_c_im[0])
    zero_state = jnp.zeros((2, bw.shape[1], SUBLANES, 2 * MODES_PER_BLOCK), F32)
    (ctx_state,) = _s5_scan(p_ctx, bw, cw, lamr, lami, zero_state, tl=64, emit_y=False)
    y_s5, _ = _s5_scan(p, bw, cw, lamr, lami, ctx_state, tl=64, emit_y=True)

    ka, kd = _hyena_filter(_filter_features(l), jnp.pad(hy_w1[0], ((0, LANES - hy_w1.shape[1]), (0, 0))),
                           row(hy_b1[0]), hy_w2[0], row(hy_b2[0]), hy_w3[0], row(hy_b3[0]),
                           row(hy_freq[0]), hy_w4[0], row(hy_deltas[0]))
    ctab, stab = _dft_tables(l)
    kk, kn = _hyena_spectrum(ctab, stab, ka, kd)
    y_hy = _hyena_conv(v, x0, kk, kn, row(hy_d[0]), ctab, stab)

    w_rh, w_rl = _split_bf16(jnp.pad(w_router[0], ((0, 0), (0, LANES - n_exp))))
    b_r = jnp.concatenate([b_router[0], jnp.full((LANES - n_exp,), NEG_BIG, F32)]).reshape(1, LANES)
    x1, h2, te, gate, cnt = _outproj(y_s5, p, y_hy, x, row(s5_d[0]), s5_w_glu[0].astype(BF16),
                                     row(s5_b_glu[0]), w_out[0].astype(BF16), ga1, row(g_norm2[0]),
                                     sh2, sc2, w_rh, w_rl, b_r)

    t = b * l
    n_blocks = t * TOP_K // ROW_BLOCK + n_exp
    counts = cnt[0, :n_exp].astype(I32)
    seg_start, first_blk, n_blk, total, pad_start, pad_len = _segments(counts)
    seg_row = jnp.broadcast_to(jnp.pad(seg_start.astype(F32), (0, LANES - n_exp)), (SUBLANES, LANES))
    pos = _slots(te.reshape(t, LANES), seg_row)
    pos_flat = pos[:, :TOP_K].reshape(-1)
    xs = _dispatch(pos_flat, pad_start, pad_len, h2.reshape(t, dm), n_blocks * ROW_BLOCK)
    hid = _expert_gateup(first_blk, n_blk, total, xs, w_gate_up[0], b_gate_up[0].reshape(n_exp, 1, -1))
    y_e = _expert_down(first_blk, n_blk, total, hid, w_down[0], b_down[0].reshape(n_exp, 1, -1))
    return _combine(pos_flat, x1, gate, ga2, row(g_final), y_e)
```

```python
import functools
import math

import jax
import jax.numpy as jnp
from jax import lax
from jax.experimental import pallas as pl
from jax.experimental.pallas import tpu as pltpu

F32 = jnp.float32
BF16 = jnp.bfloat16
I32 = jnp.int32
U32 = jnp.uint32
HIGHEST = lax.Precision.HIGHEST

GRID_W = 64
S5_H = 16
S5_P = 64
S5_RE_MAX = -1e-4
HY_BANDS = 16
N_EXPERTS = 32
TOP_K = 4
SWIGLU_LIMIT = 7.0
SWIGLU_ALPHA = 1.702
N_MOD = 6
EPS = 1e-6

LANES = 128
SUBLANES = 8
GROUPS_PER_BLOCK = LANES // S5_H
MODES_PER_BLOCK = GROUPS_PER_BLOCK * S5_P
ROW_BLOCK = 256
IN_SLOTS = 3
NEG_BIG = -1e30


def _cparams(sem, vmem_mb):
    return pltpu.CompilerParams(dimension_semantics=sem, vmem_limit_bytes=vmem_mb << 20)


def _dot(a, b):
    return jnp.dot(a, b, preferred_element_type=F32)


def _dot_hi(a, b):
    return jnp.dot(a, b, preferred_element_type=F32, precision=HIGHEST)


def _adaln_body(c_ref, w_ref, b_ref, o_ref):
    c = c_ref[...]
    s = c * jax.nn.sigmoid(c)
    o_ref[...] = _dot_hi(s, w_ref[...]) + b_ref[...]


def _adaln(cc, w, b, tn=1024):
    rows, d = cc.shape
    n = w.shape[1]
    return pl.pallas_call(
        _adaln_body,
        grid=(n // tn,),
        in_specs=[pl.BlockSpec((rows, d), lambda j: (0, 0)),
                  pl.BlockSpec((d, tn), lambda j: (0, j)),
                  pl.BlockSpec((1, tn), lambda j: (0, j))],
        out_specs=pl.BlockSpec((rows, tn), lambda j: (0, j)),
        out_shape=jax.ShapeDtypeStruct((rows, n), F32),
        compiler_params=_cparams(("arbitrary",), 40),
        name="adaln",
    )(cc, w, b)


def _norm_mod(x, g, sh, sc):
    ms = jnp.mean(x * x, axis=-1, keepdims=True)
    return (x * lax.rsqrt(ms + EPS) * g) * (1.0 + sc) + sh


def _inproj_body(x_ref, g_ref, sh_ref, sc_ref, w_ref, o_ref):
    half = x_ref.shape[1] // 2
    for r0 in (0, half):
        rs = slice(r0, r0 + half)
        h = _norm_mod(x_ref[0, rs, :], g_ref[...], sh_ref[0], sc_ref[0])
        o_ref[0, rs, :] = _dot(h.astype(BF16), w_ref[...])


def _inproj(x, g, sh, sc, w, n_cols, tm, tn=1024):
    b, l, d = x.shape
    if sh.shape[0] > 1:
        bidx = lambda n, bi, li: (bi, 0, 0)
    else:
        bidx = lambda n, bi, li: (0, 0, 0)
    return pl.pallas_call(
        _inproj_body,
        grid=(n_cols // tn, b, l // tm),
        in_specs=[pl.BlockSpec((1, tm, d), lambda n, bi, li: (bi, li, 0)),
                  pl.BlockSpec((1, d), lambda n, bi, li: (0, 0)),
                  pl.BlockSpec((1, 1, d), bidx),
                  pl.BlockSpec((1, 1, d), bidx),
                  pl.BlockSpec((d, tn), lambda n, bi, li: (0, n))],
        out_specs=pl.BlockSpec((1, tm, tn), lambda n, bi, li: (bi, li, n)),
        out_shape=jax.ShapeDtypeStruct((b, l, n_cols), F32),
        compiler_params=_cparams(("arbitrary", "arbitrary", "arbitrary"), 40),
        name="inproj",
    )(x, g, sh, sc, w)


def _s5prep_body(lre_ref, lim_ref, ls_ref, bre_ref, bim_ref, lbr_ref, lbi_ref, bbr_ref, bbi_ref):
    lr = jnp.minimum(lre_ref[...], S5_RE_MAX)
    li = lim_ref[...]
    dt = jnp.exp(ls_ref[...])
    er = jnp.exp(lr * dt)
    lbr = er * jnp.cos(li * dt)
    lbi = er * jnp.sin(li * dt)
    nr = lbr - 1.0
    den = lr * lr + li * li
    qr = (nr * lr + lbi * li) / den
    qi = (lbi * lr - nr * li) / den
    bre = bre_ref[...]
    bim = bim_ref[...]
    lbr_ref[...] = lbr
    lbi_ref[...] = lbi
    bbr_ref[...] = qr * bre - qi * bim
    bbi_ref[...] = qr * bim + qi * bre


def _s5_weights(lam_re, lam_im, log_step, b_re, b_im, c_re, c_im):
    nd, g, p = lam_re.shape
    h = S5_H
    nj = g // GROUPS_PER_BLOCK
    rep = lambda a: jnp.repeat(a.reshape(nd * g, 1, p), h, axis=1).reshape(nd * g * h, p)
    tr = lambda a: jnp.transpose(a, (0, 1, 3, 2)).reshape(nd * g * h, p)
    shp = jax.ShapeDtypeStruct((nd * g * h, p), F32)
    lbr, lbi, bbr, bbi = pl.pallas_call(
        _s5prep_body, out_shape=(shp, shp, shp, shp), name="s5prep",
    )(rep(lam_re), rep(lam_im), rep(log_step), tr(b_re), tr(b_im))
    eye = jnp.eye(GROUPS_PER_BLOCK, dtype=F32)
    bb = jnp.stack([bbr, bbi]).reshape(2, nd, nj, GROUPS_PER_BLOCK, h, p)
    bw = jnp.einsum('rdjahp,ab->djahrbp', bb, eye).reshape(nd, nj, LANES, 2 * MODES_PER_BLOCK)
    cc = jnp.stack([c_re, -c_im]).reshape(2, nd, nj, GROUPS_PER_BLOCK, h, p)
    cw = jnp.einsum('rdjahp,ab->djrapbh', cc, eye).reshape(nd, nj, 2 * MODES_PER_BLOCK, LANES)
    pick = lambda a: a.reshape(nd, g, h, p)[:, :, 0, :].reshape(nd, nj, 1, MODES_PER_BLOCK)
    return bw.astype(BF16), cw.astype(BF16), pick(lbr), pick(lbi)


def _s5_body(*refs, tl, nt, nj, emit_y):
    if emit_y:
        (u_hbm, bw_ref, cw_ref, lr_ref, li_ref, init_ref, y_hbm, fin_ref,
         u_scr, bu_scr, st_scr, in_sem, y_scr, out_sem) = refs
    else:
        (u_hbm, bw_ref, lr_ref, li_ref, init_ref, fin_ref, u_scr, bu_scr, st_scr, in_sem) = refs
    g = pl.program_id(0)
    ng = pl.num_programs(0)
    d = g // nt
    t = g % nt
    slot = g % 2
    nb = SUBLANES
    mpb = MODES_PER_BLOCK
    c = nj * LANES
    rows = tl * nb

    def tile_of(step):
        dd = step // nt
        tt = step % nt
        return dd, tt + dd * (nt - 1 - 2 * tt)

    def in_copies(step, sl):
        _, tile = tile_of(step)
        return [pltpu.make_async_copy(u_hbm.at[b, pl.ds(tile * tl, tl), pl.ds(0, c)],
                                      u_scr.at[sl, :, b, :], in_sem.at[sl]) for b in range(nb)]

    def out_copies(step, sl):
        dd, tile = tile_of(step)
        return [pltpu.make_async_copy(y_scr.at[sl, :, b, :], y_hbm.at[dd, b, pl.ds(tile * tl, tl), :],
                                      out_sem.at[sl]) for b in range(nb)]

    @pl.when(g == 0)
    def _():
        for cp in in_copies(0, 0):
            cp.start()

    @pl.when(g + 1 < ng)
    def _():
        for cp in in_copies(g + 1, (g + 1) % 2):
            cp.start()

    for cp in in_copies(g, slot):
        cp.wait()

    if emit_y:
        @pl.when(g >= 2)
        def _():
            for cp in out_copies(g - 2, slot):
                cp.wait()

    @pl.when(t == 0)
    def _():
        st_scr[...] = init_ref[0]

    def project(j):
        s = u_scr[slot, :, :, j * LANES:(j + 1) * LANES].reshape(rows, LANES)
        bu_scr[j % 3] = _dot(s.astype(BF16), bw_ref[0, j])

    def readout(j):
        y = _dot(bu_scr[j % 3].astype(BF16), cw_ref[0, j])
        y_scr[slot, :, :, j * LANES:(j + 1) * LANES] = y.reshape(tl, nb, LANES)

    project(0)
    for j in range(nj):
        if j + 1 < nj:
            project(j + 1)
        if emit_y and j >= 1:
            readout(j - 1)
        lr = jnp.broadcast_to(lr_ref[0, j], (nb, mpb))
        li = jnp.broadcast_to(li_ref[0, j], (nb, mpb))
        cur = j % 3
        xr = st_scr[j, :, 0:mpb]
        xi = st_scr[j, :, mpb:2 * mpb]
        for s in range(tl):
            r0 = pl.multiple_of((s + d * (tl - 1 - 2 * s)) * nb, nb)
            br = bu_scr[cur, pl.ds(r0, nb), 0:mpb]
            bi = bu_scr[cur, pl.ds(r0, nb), mpb:2 * mpb]
            xr, xi = lr * xr - li * xi + br, lr * xi + li * xr + bi
            bu_scr[cur, pl.ds(r0, nb), 0:mpb] = xr
            bu_scr[cur, pl.ds(r0, nb), mpb:2 * mpb] = xi
        st_scr[j, :, 0:mpb] = xr
        st_scr[j, :, mpb:2 * mpb] = xi

    if emit_y:
        readout(nj - 1)
        for cp in out_copies(g, slot):
            cp.start()

        @pl.when(g == ng - 1)
        def _():
            for cp in out_copies(g - 1, 1 - slot):
                cp.wait()
            for cp in out_copies(g, slot):
                cp.wait()

    @pl.when(t == nt - 1)
    def _():
        fin_ref[0] = st_scr[...]


def _s5_scan(u_arr, bw, cw, lamr, lami, init, tl, emit_y):
    b, l = u_arr.shape[:2]
    assert b == SUBLANES and l % tl == 0
    nj = bw.shape[1]
    c = nj * LANES
    nt = l // tl
    m2 = 2 * MODES_PER_BLOCK
    wmap = lambda g: (g // nt, 0, 0, 0)
    in_specs = [pl.BlockSpec(memory_space=pl.ANY), pl.BlockSpec((1, nj, LANES, m2), wmap)]
    args = [u_arr, bw]
    if emit_y:
        in_specs.append(pl.BlockSpec((1, nj, m2, LANES), wmap))
        args.append(cw)
    in_specs += [pl.BlockSpec((1, nj, 1, MODES_PER_BLOCK), wmap),
                 pl.BlockSpec((1, nj, 1, MODES_PER_BLOCK), wmap),
                 pl.BlockSpec((1, nj, SUBLANES, m2), wmap)]
    args += [lamr, lami, init]
    fin_spec = pl.BlockSpec((1, nj, SUBLANES, m2), wmap)
    fin_shape = jax.ShapeDtypeStruct((2, nj, SUBLANES, m2), F32)
    scratch = [pltpu.VMEM((2, tl, SUBLANES, c), F32),
               pltpu.VMEM((3, tl * SUBLANES, m2), F32),
               pltpu.VMEM((nj, SUBLANES, m2), F32),
               pltpu.SemaphoreType.DMA((2,))]
    if emit_y:
        out_specs = [pl.BlockSpec(memory_space=pl.ANY), fin_spec]
        out_shape = [jax.ShapeDtypeStruct((2, b, l, c), F32), fin_shape]
        scratch += [pltpu.VMEM((2, tl, SUBLANES, c), F32), pltpu.SemaphoreType.DMA((2,))]
    else:
        out_specs = [fin_spec]
        out_shape = [fin_shape]
    return pl.pallas_call(
        functools.partial(_s5_body, tl=tl, nt=nt, nj=nj, emit_y=emit_y),
        grid=(2 * nt,),
        in_specs=in_specs, out_specs=out_specs, out_shape=out_shape,
        scratch_shapes=scratch,
        compiler_params=_cparams(("arbitrary",), 48),
        name="s5scan_lat" if emit_y else "s5scan_ctx",
    )(*args)


def _hyfilt_body(z_ref, w1_ref, b1_ref, w2_ref, b2_ref, w3_ref, b3_ref, fr_ref,
                 w4f_ref, w4b_ref, df_ref, db_ref, ka_ref, kd_ref, h_scr, *, length):
    z = z_ref[...]

    @pl.when(pl.program_id(0) == 0)
    def _():
        fr = fr_ref[...]
        h = jnp.sin(fr * (_dot_hi(z, w1_ref[...]) + b1_ref[...]))
        h = jnp.sin(fr * (_dot_hi(h, w2_ref[...]) + b2_ref[...]))
        h_scr[...] = jnp.sin(fr * (_dot_hi(h, w3_ref[...]) + b3_ref[...]))

    h = h_scr[...]
    t = z[:, 0:1]
    hf = _dot_hi(h, w4f_ref[...]) * jnp.exp(-t * jnp.abs(df_ref[...]))
    hb = _dot_hi(h, w4b_ref[...]) * jnp.exp(-t * jnp.abs(db_ref[...]))
    row = lax.broadcasted_iota(I32, hf.shape, 0)
    hb = jnp.where(row < length - 1, hb, 0.0)
    norm = jnp.sum(jnp.abs(hf), axis=0, keepdims=True) + jnp.sum(jnp.abs(hb), axis=0, keepdims=True)
    kf = hf / norm
    kb = jnp.where(row >= 1, pltpu.roll(hb, 1, axis=0), 0.0) / norm
    ka_ref[...] = kf + kb
    kd_ref[...] = kf - kb


def _hyena_filter(z, w1, b1, w2, b2, w3, b3, fr, w4, deltas, ct=256):
    length, ze = z.shape
    order = w2.shape[0]
    c = w4.shape[1] // 2
    nct = c // ct
    full = lambda shape: pl.BlockSpec(shape, lambda i: (0, 0))
    shp = jax.ShapeDtypeStruct((length, c), F32)
    return pl.pallas_call(
        functools.partial(_hyfilt_body, length=length),
        grid=(nct,),
        in_specs=[full((length, ze)), full((ze, order)), full((1, order)),
                  full((order, order)), full((1, order)), full((order, order)), full((1, order)),
                  full((1, order)),
                  pl.BlockSpec((order, ct), lambda i: (0, i)),
                  pl.BlockSpec((order, ct), lambda i: (0, nct + i)),
                  pl.BlockSpec((1, ct), lambda i: (0, i)),
                  pl.BlockSpec((1, ct), lambda i: (0, nct + i))],
        out_specs=[pl.BlockSpec((length, ct), lambda i: (0, i))] * 2,
        out_shape=[shp, shp],
        scratch_shapes=[pltpu.VMEM((length, order), F32)],
        compiler_params=_cparams(("arbitrary",), 48),
        name="hyena_filter",
    )(z, w1, b1, w2, b2, w3, b3, fr, w4, w4, deltas, deltas)


def _dfttab_body(ca_ref, sa_ref, cb_ref, sb_ref, c_ref, s_ref):
    ca = ca_ref[0]
    sa = sa_ref[0]
    cb = cb_ref[...]
    sb = sb_ref[...]
    c_ref[...] = (ca * cb - sa * sb).astype(c_ref.dtype)
    s_ref[...] = (sa * cb + ca * sb).astype(s_ref.dtype)


def _dft_tables(length, r=128):
    n = 2 * length
    s = jnp.arange(length, dtype=I32)[None, :]
    ang = lambda k: (2.0 * math.pi / n) * ((k * s) % n).astype(F32)
    aa = ang(jnp.arange(length // r, dtype=I32)[:, None] * r)
    ab = ang(jnp.arange(r, dtype=I32)[:, None])
    rows3 = lambda a: a.reshape(length // r, 1, length)
    shp = jax.ShapeDtypeStruct((length, length), BF16)
    return pl.pallas_call(
        _dfttab_body,
        grid=(length // r,),
        in_specs=[pl.BlockSpec((None, 1, length), lambda i: (i, 0, 0)),
                  pl.BlockSpec((None, 1, length), lambda i: (i, 0, 0)),
                  pl.BlockSpec((r, length), lambda i: (0, 0)),
                  pl.BlockSpec((r, length), lambda i: (0, 0))],
        out_specs=[pl.BlockSpec((r, length), lambda i: (i, 0))] * 2,
        out_shape=[shp, shp],
        compiler_params=_cparams(("arbitrary",), 32),
        name="dft_tables",
    )(rows3(jnp.cos(aa)), rows3(jnp.sin(aa)), jnp.cos(ab), jnp.sin(ab))


def _hyspec_body(c_ref, s_ref, ka_ref, kd_ref, kk_ref, kn_ref, *, length, fq):
    q = pl.program_id(1)
    nq = length // fq
    n = 2.0 * length

    @pl.when(q < nq)
    def _():
        grow = q * fq + lax.broadcasted_iota(I32, (fq, 1), 0)
        scale = jnp.where(grow == 0, 1.0 / n, 2.0 / n)
        kk_ref[...] = _dot(c_ref[...], ka_ref[...].astype(BF16)) * scale

    @pl.when(q >= nq)
    def _():
        kk_ref[...] = _dot(s_ref[...], kd_ref[...].astype(BF16)) * (2.0 / n)

    @pl.when(q == 0)
    def _():
        a = ka_ref[...]
        row = lax.broadcasted_iota(I32, a.shape, 0)
        sgn = jnp.where(row % 2 == 0, 1.0, -1.0)
        kn = jnp.sum(a * sgn, axis=0, keepdims=True) * (1.0 / n)
        kn_ref[...] = jnp.broadcast_to(kn, kn_ref.shape)


def _hyena_spectrum(ctab, stab, ka, kd, ct=256, fq=512):
    length, c = ka.shape
    nq = length // fq
    return pl.pallas_call(
        functools.partial(_hyspec_body, length=length, fq=fq),
        grid=(c // ct, 2 * nq),
        in_specs=[pl.BlockSpec((fq, length), lambda i, q: (jnp.minimum(q, nq - 1), 0)),
                  pl.BlockSpec((fq, length), lambda i, q: (jnp.maximum(q - nq, 0), 0)),
                  pl.BlockSpec((length, ct), lambda i, q: (0, i)),
                  pl.BlockSpec((length, ct), lambda i, q: (0, i))],
        out_specs=[pl.BlockSpec((fq, ct), lambda i, q: (q, i)),
                   pl.BlockSpec((SUBLANES, ct), lambda i, q: (0, i))],
        out_shape=[jax.ShapeDtypeStruct((2 * length, c), F32), jax.ShapeDtypeStruct((SUBLANES, c), F32)],
        compiler_params=_cparams(("arbitrary", "arbitrary"), 48),
        name="hyena_spectrum",
    )(ctab, stab, ka, kd)


def _inproj_hy_body(x_ref, g_ref, sh_ref, sc_ref, wv_ref, w1_ref, w0_ref, cv_ref, c1_ref, c0_ref,
                    bv_ref, b1_ref, b0_ref, v_ref, x0_ref):
    half = x_ref.shape[1] // 2
    row = lax.broadcasted_iota(I32, (half, 1), 0) % GRID_W

    def sconv(a, w_ref, b_ref):
        prev = jnp.where(row == 0, 0.0, pltpu.roll(a, 1, axis=0))
        nxt = jnp.where(row == GRID_W - 1, 0.0, pltpu.roll(a, half - 1, axis=0))
        return prev * w_ref[0:1, :] + a * w_ref[1:2, :] + nxt * w_ref[2:3, :] + b_ref[...]

    for r0 in (0, half):
        rs = slice(r0, r0 + half)
        h = _norm_mod(x_ref[0, rs, :], g_ref[...], sh_ref[0], sc_ref[0]).astype(BF16)
        zv = sconv(_dot(h, wv_ref[...]), cv_ref, bv_ref)
        z1 = sconv(_dot(h, w1_ref[...]), c1_ref, b1_ref)
        v_ref[0, rs, :] = (zv * z1).astype(v_ref.dtype)
        x0_ref[0, rs, :] = sconv(_dot(h, w0_ref[...]), c0_ref, b0_ref).astype(x0_ref.dtype)


def _inproj_hyena(x, g, sh, sc, w, conv_w, conv_b, c, first_col, tm=512, ct=256):
    b, l, d = x.shape
    assert tm % (2 * GRID_W) == 0
    nct = c // ct
    base = first_col // ct
    tokmap = lambda ci, bi, li: (bi, li, 0)
    perb = pl.BlockSpec((1, 1, d), lambda ci, bi, li: (bi, 0, 0))
    wspec = lambda k: pl.BlockSpec((d, ct), lambda ci, bi, li: (0, base + k * nct + ci))
    cspec = lambda k: pl.BlockSpec((3, ct), lambda ci, bi, li: (0, k * nct + ci))
    bspec = lambda k: pl.BlockSpec((1, ct), lambda ci, bi, li: (0, k * nct + ci))
    shp = jax.ShapeDtypeStruct((b, l, c), BF16)
    return pl.pallas_call(
        _inproj_hy_body,
        grid=(nct, b, l // tm),
        in_specs=[pl.BlockSpec((1, tm, d), tokmap), pl.BlockSpec((1, d), lambda ci, bi, li: (0, 0)), perb, perb,
                  wspec(0), wspec(1), wspec(2), cspec(0), cspec(1), cspec(2), bspec(0), bspec(1), bspec(2)],
        out_specs=[pl.BlockSpec((1, tm, ct), lambda ci, bi, li: (bi, li, ci))] * 2,
        out_shape=[shp, shp],
        compiler_params=_cparams(("arbitrary", "arbitrary", "arbitrary"), 40),
        name="inproj_hyena",
    )(x, g, sh, sc, w, w, w, conv_w, conv_w, conv_w, conv_b, conv_b, conv_b)


def _hyconv_body(v_ref, x0_ref, kk_ref, kn_ref, d_ref, c_hbm, s_hbm, o_ref, c_scr, s_scr, sem, *, length, fq):
    first = jnp.logical_and(pl.program_id(0) == 0, pl.program_id(1) == 0)

    @pl.when(first)
    def _():
        cc = pltpu.make_async_copy(c_hbm, c_scr, sem.at[0])
        cs = pltpu.make_async_copy(s_hbm, s_scr, sem.at[1])
        cc.start()
        cs.start()
        cc.wait()
        cs.wait()

    vb = v_ref[0]
    v = vb.astype(F32)
    row = lax.broadcasted_iota(I32, (length, 1), 0)
    sgn = jnp.where(row % 2 == 0, 1.0, -1.0)
    vn = jnp.sum(v * sgn, axis=0, keepdims=True)
    acc = sgn * (vn * kn_ref[0:1, :])
    for q in range(length // fq):
        lo, hi = q * fq, (q + 1) * fq
        vr = _dot(c_scr[lo:hi, :], vb)
        vs = _dot(s_scr[lo:hi, :], vb)
        kr = kk_ref[lo:hi, :]
        ks = kk_ref[length + lo:length + hi, :]
        pr = vr * kr - vs * ks
        ps = vr * ks + vs * kr
        acc = acc + _dot(c_scr[:, lo:hi], pr.astype(BF16)) + _dot(s_scr[:, lo:hi], ps.astype(BF16))
    o_ref[0] = ((acc + v * d_ref[...]) * x0_ref[0].astype(F32)).astype(o_ref.dtype)


def _hyena_conv(v, x0, kk, kn, d_skip, ctab, stab, ct=256, fq=512):
    b, l, c = v.shape
    return pl.pallas_call(
        functools.partial(_hyconv_body, length=l, fq=fq),
        grid=(c // ct, b),
        in_specs=[pl.BlockSpec((1, l, ct), lambda ci, bi: (bi, 0, ci)),
                  pl.BlockSpec((1, l, ct), lambda ci, bi: (bi, 0, ci)),
                  pl.BlockSpec((2 * l, ct), lambda ci, bi: (0, ci)),
                  pl.BlockSpec((SUBLANES, ct), lambda ci, bi: (0, ci)),
                  pl.BlockSpec((1, ct), lambda ci, bi: (0, ci)),
                  pl.BlockSpec(memory_space=pl.ANY),
                  pl.BlockSpec(memory_space=pl.ANY)],
        out_specs=pl.BlockSpec((1, l, ct), lambda ci, bi: (bi, 0, ci)),
        out_shape=jax.ShapeDtypeStruct((b, l, c), BF16),
        scratch_shapes=[pltpu.VMEM((l, l), BF16), pltpu.VMEM((l, l), BF16),
                        pltpu.SemaphoreType.DMA((2,))],
        compiler_params=_cparams(("arbitrary", "arbitrary"), 48),
        name="hyena_conv",
    )(v, x0, kk, kn, d_skip, ctab, stab)


def _split_bf16(a):
    hi = a.astype(BF16)
    return hi, (a - hi.astype(F32)).astype(BF16)


def _outproj_body(y_ref, u_ref, yh_ref, x_ref, d_ref, wg_ref, bg_ref, wo_ref, ga_ref, g2_ref,
                  sh_ref, sc_ref, wrh_ref, wrl_ref, br_ref, x1_ref, hp_ref, te_ref, gt_ref, cnt_ref, *, sub):
    tm, c = u_ref.shape[1], u_ref.shape[2]
    dm = x_ref.shape[2]

    @pl.when(jnp.logical_and(pl.program_id(0) == 0, pl.program_id(1) == 0))
    def _():
        cnt_ref[...] = jnp.zeros_like(cnt_ref)

    lane = lax.broadcasted_iota(I32, (sub, LANES), 1)
    total = jnp.zeros((1, LANES), F32)
    for r0 in range(0, tm, sub):
        rs = slice(r0, r0 + sub)
        ys = y_ref[0, 0, rs, :] + y_ref[1, 0, rs, :] + u_ref[0, rs, :] * d_ref[...]
        ge = 0.5 * ys * (1.0 + lax.erf(ys * (1.0 / math.sqrt(2.0))))
        gl = ge * jax.nn.sigmoid(_dot(ge.astype(BF16), wg_ref[...]) + bg_ref[...])
        mix = _dot(gl.astype(BF16), wo_ref[0:c, :]) + _dot(yh_ref[0, rs, :], wo_ref[c:2 * c, :])
        x1 = x_ref[0, rs, :] + ga_ref[0] * mix
        x1_ref[0, rs, :] = x1
        h2 = _norm_mod(x1, g2_ref[...], sh_ref[0], sc_ref[0])
        hp_ref[0, rs] = h2.astype(BF16).reshape(sub, dm // LANES, LANES)

        hh, hl = _split_bf16(h2)
        logits = _dot(hh, wrh_ref[...]) + (_dot(hh, wrl_ref[...]) + _dot(hl, wrh_ref[...])) + br_ref[...]
        work = logits
        vals, idxs = [], []
        for _ in range(TOP_K):
            m = jnp.max(work, axis=-1, keepdims=True)
            ix = jnp.min(jnp.where(work == m, lane, LANES), axis=-1, keepdims=True)
            vals.append(m)
            idxs.append(ix)
            work = jnp.where(lane == ix, NEG_BIG, work)
        ex = [jnp.exp(v - vals[0]) for v in vals]
        den = ex[0]
        for e in ex[1:]:
            den = den + e
        te = jnp.zeros((sub, LANES), I32)
        gt = jnp.zeros((sub, LANES), F32)
        mh = jnp.zeros((sub, LANES), F32)
        for k in range(TOP_K):
            te = jnp.where(lane == k, idxs[k], te)
            gt = jnp.where(lane == k, ex[k] / den, gt)
            mh = mh + (lane == idxs[k]).astype(F32)
        te_ref[0, rs, :] = te
        gt_ref[0, rs, :] = gt
        total = total + jnp.sum(mh, axis=0, keepdims=True)

    cnt_ref[...] += jnp.broadcast_to(total, cnt_ref.shape)


def _outproj(y, p, yh, x, d_skip, w_glu, b_glu, w_out, ga1, g2, sh2, sc2, w_rh, w_rl, b_r, tm=256, sub=128):
    b, l, dm = x.shape
    c = yh.shape[2]
    tok = lambda width: pl.BlockSpec((1, tm, width), lambda bi, li: (bi, li, 0))
    full = lambda shape: pl.BlockSpec(shape, lambda bi, li: (0,) * len(shape))
    perb = pl.BlockSpec((1, 1, dm), lambda bi, li: (bi, 0, 0))
    return pl.pallas_call(
        functools.partial(_outproj_body, sub=sub),
        grid=(b, l // tm),
        in_specs=[pl.BlockSpec((2, 1, tm, c), lambda bi, li: (0, bi, li, 0)),
                  tok(c), tok(c), tok(dm),
                  full((1, c)), full((c, c)), full((1, c)), full((2 * c, dm)),
                  perb, full((1, dm)), perb, perb,
                  full((dm, LANES)), full((dm, LANES)), full((1, LANES))],
        out_specs=[tok(dm), pl.BlockSpec((1, tm, dm // LANES, LANES), lambda bi, li: (bi, li, 0, 0)),
                   tok(LANES), tok(LANES), full((SUBLANES, LANES))],
        out_shape=[jax.ShapeDtypeStruct((b, l, dm), F32),
                   jax.ShapeDtypeStruct((b, l, dm // LANES, LANES), BF16),
                   jax.ShapeDtypeStruct((b, l, LANES), I32),
                   jax.ShapeDtypeStruct((b, l, LANES), F32),
                   jax.ShapeDtypeStruct((SUBLANES, LANES), F32)],
        compiler_params=_cparams(("arbitrary", "arbitrary"), 56),
        name="outproj_router",
    )(y, p, yh, x, d_skip, w_glu, b_glu, w_out, ga1, g2, sh2, sc2, w_rh, w_rl, b_r)


def _slots_body(te_ref, seg_ref, pos_ref, carry):
    @pl.when(pl.program_id(0) == 0)
    def _():
        carry[...] = jnp.zeros_like(carry)

    te = te_ref[...]
    tm = te.shape[0]
    lane = lax.broadcasted_iota(I32, (tm, LANES), 1)
    sel = [lane == te[:, k:k + 1] for k in range(TOP_K)]
    mh = jnp.zeros((tm, LANES), F32)
    for s in sel:
        mh = mh + s.astype(F32)
    r = lax.broadcasted_iota(I32, (tm, tm), 0)
    cc = lax.broadcasted_iota(I32, (tm, tm), 1)
    strict_lower = (r > cc).astype(BF16)
    slot = _dot(strict_lower, mh.astype(BF16)) + carry[0:1, :] + seg_ref[0:1, :]
    pos = jnp.zeros((tm, LANES), F32)
    for k in range(TOP_K):
        pk = jnp.sum(jnp.where(sel[k], slot, 0.0), axis=-1, keepdims=True)
        pos = jnp.where(lane == k, pk, pos)
    pos_ref[...] = pos.astype(I32)
    carry[...] += jnp.broadcast_to(jnp.sum(mh, axis=0, keepdims=True), carry.shape)


def _slots(te, seg_start, tm=512):
    t = te.shape[0]
    return pl.pallas_call(
        _slots_body,
        grid=(t // tm,),
        in_specs=[pl.BlockSpec((tm, LANES), lambda i: (i, 0)),
                  pl.BlockSpec((SUBLANES, LANES), lambda i: (0, 0))],
        out_specs=pl.BlockSpec((tm, LANES), lambda i: (i, 0)),
        out_shape=jax.ShapeDtypeStruct((t, LANES), I32),
        scratch_shapes=[pltpu.VMEM((SUBLANES, LANES), F32)],
        compiler_params=_cparams(("arbitrary",), 32),
        name="moe_slots",
    )(te, seg_start)


PAD_CHUNKS = tuple(ROW_BLOCK >> k for k in range(1, ROW_BLOCK.bit_length()))


def _row_copy(src, src_row, dst, dst_row, sem):
    return pltpu.make_async_copy(src.at[pl.ds(src_row, 1)], dst.at[pl.ds(dst_row, 1)], sem)


def _dispatch_body(pos_ref, ps_ref, pn_ref, h_ref, xs_out, zbuf, sem, zsem, *, tm, n_exp, n_blocks):
    @pl.when(pl.program_id(0) == 0)
    def _():
        zbuf[...] = jnp.zeros_like(zbuf)

        def copies(e):
            start = ps_ref[e]
            length = pn_ref[e]
            out = []
            for size in PAD_CHUNKS:
                done = length & (-2 * size)
                dst = xs_out.at[pl.ds(start + done, size)]
                out.append(((length & size) != 0, pltpu.make_async_copy(zbuf.at[pl.ds(0, size)], dst, zsem)))
            return out

        def fill(e, carry):
            for on, cp in copies(e):
                pl.when(on)(cp.start)
            return carry

        def drain(e, carry):
            for on, cp in copies(e):
                pl.when(on)(cp.wait)
            return carry

        lax.fori_loop(0, n_exp, fill, 0)
        lax.fori_loop(0, n_exp, drain, 0)

        def tail(blk):
            rows = [blk * ROW_BLOCK + part * PAD_CHUNKS[0] for part in range(ROW_BLOCK // PAD_CHUNKS[0])]
            return [pltpu.make_async_copy(zbuf, xs_out.at[pl.ds(r, PAD_CHUNKS[0])], zsem) for r in rows]

        def fill_tail(blk, carry):
            for cp in tail(blk):
                cp.start()
            return carry

        def drain_tail(blk, carry):
            for cp in tail(blk):
                cp.wait()
            return carry

        first_unused = (ps_ref[n_exp - 1] + pn_ref[n_exp - 1]) // ROW_BLOCK
        lax.fori_loop(first_unused, n_blocks, fill_tail, 0)
        lax.fori_loop(first_unused, n_blocks, drain_tail, 0)

    base = pl.program_id(0) * (tm * TOP_K)

    def issue(r, carry):
        for k in range(TOP_K):
            _row_copy(h_ref, r, xs_out, pos_ref[base + r * TOP_K + k], sem).start()
        return carry

    lax.fori_loop(0, tm, issue, 0, unroll=8)

    def drain_rows(r, carry):
        for k in range(TOP_K):
            _row_copy(h_ref, 0, xs_out, 0, sem).wait()
        return carry

    lax.fori_loop(0, tm, drain_rows, 0, unroll=8)


def _dispatch(pos_flat, pad_start, pad_len, hp, n_rows, tm=256):
    t = hp.shape[0]
    tile = hp.shape[1:]
    return pl.pallas_call(
        functools.partial(_dispatch_body, tm=tm, n_exp=pad_start.shape[0], n_blocks=n_rows // ROW_BLOCK),
        grid_spec=pltpu.PrefetchScalarGridSpec(
            num_scalar_prefetch=3,
            grid=(t // tm,),
            in_specs=[pl.BlockSpec((tm,) + tile, lambda i, pos, ps, pn: (i, 0, 0))],
            out_specs=pl.BlockSpec(memory_space=pl.ANY),
            scratch_shapes=[pltpu.VMEM((PAD_CHUNKS[0],) + tile, hp.dtype),
                            pltpu.SemaphoreType.DMA(()), pltpu.SemaphoreType.DMA(())]),
        out_shape=jax.ShapeDtypeStruct((n_rows,) + tile, hp.dtype),
        compiler_params=_cparams(("arbitrary",), 32),
        name="moe_dispatch",
    )(pos_flat, pad_start, pad_len, hp)


def _stream_expert_blocks(first, count, total, n_blocks, src_hbm, dst_hbm, col0, ibuf, obuf, isem, osem, compute):
    e = pl.program_id(1)
    rb = ROW_BLOCK
    n_in = ibuf.shape[0]

    def rows(gs):
        return pl.ds(pl.multiple_of(gs * rb, rb), rb)

    def in_copy(gs):
        slot = gs % n_in
        return pltpu.make_async_copy(src_hbm.at[rows(gs)], ibuf.at[slot], isem.at[slot])

    def out_copy(gs, slot):
        dst = dst_hbm.at[rows(gs)] if col0 is None else dst_hbm.at[rows(gs), pl.ds(col0, obuf.shape[2])]
        return pltpu.make_async_copy(obuf.at[slot], dst, osem.at[slot])

    @pl.when(e == 0)
    def _():
        for gs in range(n_in - 1):
            pl.when(gs < total)(in_copy(gs).start)

    def block(s, carry):
        gs = first + s
        slot = gs % 2

        @pl.when(gs + n_in - 1 < total)
        def _():
            in_copy(gs + n_in - 1).start()

        in_copy(gs).wait()

        @pl.when(gs >= 2)
        def _():
            out_copy(gs - 2, slot).wait()

        compute(ibuf.at[gs % n_in], obuf.at[slot])
        out_copy(gs, slot).start()
        return carry

    lax.fori_loop(0, count, block, 0)

    @pl.when(e == pl.num_programs(1) - 1)
    def _():
        @pl.when(total >= 2)
        def _():
            out_copy(total - 2, total % 2).wait()

        out_copy(total - 1, (total - 1) % 2).wait()
        obuf[0] = jnp.zeros(obuf.shape[1:], obuf.dtype)

        def fill(gs, carry):
            out_copy(gs, 0).start()
            return carry

        def drain(gs, carry):
            out_copy(gs, 0).wait()
            return carry

        lax.fori_loop(total, n_blocks, fill, 0)
        lax.fori_loop(total, n_blocks, drain, 0)


def _gateup_body(fs_ref, ns_ref, tot_ref, xs_hbm, wg_ref, wu_ref, bg_ref, bu_ref, h_hbm,
                 wg_s, wu_s, xbuf, obuf, xsem, osem, *, n_blocks):
    e = pl.program_id(1)
    tn = obuf.shape[2]
    half = tn // 2

    @pl.when(ns_ref[e] > 0)
    def _():
        wg_s[...] = wg_ref[...].astype(BF16)
        wu_s[...] = wu_ref[...].astype(BF16)

    def compute(x_ref, o_ref):
        x = x_ref[...].reshape(ROW_BLOCK, wg_s.shape[0])
        for c0 in (0, half):
            cs = slice(c0, c0 + half)
            g = _dot(x, wg_s[:, cs]) + bg_ref[:, cs]
            u = _dot(x, wu_s[:, cs]) + bu_ref[:, cs]
            g = jnp.minimum(g, SWIGLU_LIMIT)
            u = jnp.clip(u, -SWIGLU_LIMIT, SWIGLU_LIMIT)
            o_ref[:, cs] = ((u + 1.0) * (g * jax.nn.sigmoid(SWIGLU_ALPHA * g))).astype(o_ref.dtype)

    col0 = pl.multiple_of(pl.program_id(0) * tn, tn)
    _stream_expert_blocks(fs_ref[e], ns_ref[e], tot_ref[0], n_blocks, xs_hbm, h_hbm, col0,
                          xbuf, obuf, xsem, osem, compute)


def _expert_gateup(first_blk, n_blk, total, xs, w_gate_up, b_gate_up, tn=1024):
    n_exp, dm, ff2 = w_gate_up.shape
    ff = ff2 // 2
    nr = xs.shape[0]
    nn = ff // tn
    return pl.pallas_call(
        functools.partial(_gateup_body, n_blocks=nr // ROW_BLOCK),
        grid_spec=pltpu.PrefetchScalarGridSpec(
            num_scalar_prefetch=3,
            grid=(nn, n_exp),
            in_specs=[pl.BlockSpec(memory_space=pl.ANY),
                      pl.BlockSpec((None, dm, tn), lambda j, e, fs, ns, tot: (e, 0, j)),
                      pl.BlockSpec((None, dm, tn), lambda j, e, fs, ns, tot: (e, 0, nn + j)),
                      pl.BlockSpec((None, 1, tn), lambda j, e, fs, ns, tot: (e, 0, j)),
                      pl.BlockSpec((None, 1, tn), lambda j, e, fs, ns, tot: (e, 0, nn + j))],
            out_specs=pl.BlockSpec(memory_space=pl.ANY),
            scratch_shapes=[pltpu.VMEM((dm, tn), BF16), pltpu.VMEM((dm, tn), BF16),
                            pltpu.VMEM((IN_SLOTS, ROW_BLOCK) + xs.shape[1:], xs.dtype),
                            pltpu.VMEM((2, ROW_BLOCK, tn), BF16),
                            pltpu.SemaphoreType.DMA((IN_SLOTS,)), pltpu.SemaphoreType.DMA((2,))]),
        out_shape=jax.ShapeDtypeStruct((nr, ff), BF16),
        compiler_params=_cparams(("arbitrary", "arbitrary"), 56),
        name="moe_gateup",
    )(first_blk, n_blk, total, xs, w_gate_up, w_gate_up, b_gate_up, b_gate_up)


def _down_body(fs_ref, ns_ref, tot_ref, h_hbm, w_ref, b_ref, y_hbm, w_s, hbuf, obuf, hsem, osem, *, n_blocks):
    e = pl.program_id(1)

    @pl.when(ns_ref[e] > 0)
    def _():
        w_s[...] = w_ref[...].astype(BF16)

    def compute(h_ref, o_ref):
        y = _dot(h_ref[...], w_s[...]) + b_ref[...]
        o_ref[...] = y.astype(o_ref.dtype).reshape(o_ref.shape)

    _stream_expert_blocks(fs_ref[e], ns_ref[e], tot_ref[0], n_blocks, h_hbm, y_hbm, None,
                          hbuf, obuf, hsem, osem, compute)


def _expert_down(first_blk, n_blk, total, h, w_down, b_down):
    n_exp, ff, dm = w_down.shape
    nr = h.shape[0]
    tile = (dm // LANES, LANES)
    return pl.pallas_call(
        functools.partial(_down_body, n_blocks=nr // ROW_BLOCK),
        grid_spec=pltpu.PrefetchScalarGridSpec(
            num_scalar_prefetch=3,
            grid=(1, n_exp),
            in_specs=[pl.BlockSpec(memory_space=pl.ANY),
                      pl.BlockSpec((None, ff, dm), lambda j, e, fs, ns, tot: (e, 0, 0)),
                      pl.BlockSpec((None, 1, dm), lambda j, e, fs, ns, tot: (e, 0, 0))],
            out_specs=pl.BlockSpec(memory_space=pl.ANY),
            scratch_shapes=[pltpu.VMEM((ff, dm), BF16),
                            pltpu.VMEM((IN_SLOTS, ROW_BLOCK, ff), h.dtype), pltpu.VMEM((2, ROW_BLOCK) + tile, BF16),
                            pltpu.SemaphoreType.DMA((IN_SLOTS,)), pltpu.SemaphoreType.DMA((2,))]),
        out_shape=jax.ShapeDtypeStruct((nr,) + tile, BF16),
        compiler_params=_cparams(("arbitrary", "arbitrary"), 56),
        name="moe_down",
    )(first_blk, n_blk, total, h, w_down, b_down)


def _combine_body(pos_ref, x1_ref, gt_ref, ga_ref, gf_ref, y_hbm, o_ref, ybuf, sem, *, tm):
    i = pl.program_id(0)
    n = pl.num_programs(0)

    def gather(tile, slot):
        def issue(r, carry):
            for k in range(TOP_K):
                p = pos_ref[(tile * tm + r) * TOP_K + k]
                _row_copy(y_hbm, p, ybuf.at[slot, k], r, sem.at[slot]).start()
            return carry
        lax.fori_loop(0, tm, issue, 0, unroll=8)

    @pl.when(i == 0)
    def _():
        gather(0, 0)

    @pl.when(i + 1 < n)
    def _():
        gather(i + 1, (i + 1) % 2)

    slot = i % 2

    def drain(r, carry):
        for k in range(TOP_K):
            _row_copy(y_hbm, 0, ybuf.at[slot, k], 0, sem.at[slot]).wait()
        return carry

    lax.fori_loop(0, tm, drain, 0, unroll=8)

    gt = gt_ref[0]
    dm = x1_ref.shape[2]
    rows_of = lambda k: ybuf[slot, k].reshape(tm, dm).astype(F32)
    acc = gt[:, 0:1] * rows_of(0)
    for k in range(1, TOP_K):
        acc = acc + gt[:, k:k + 1] * rows_of(k)
    x2 = x1_ref[0] + ga_ref[0] * acc
    ms = jnp.mean(x2 * x2, axis=-1, keepdims=True)
    o_ref[0] = x2 * lax.rsqrt(ms + EPS) * gf_ref[...]


def _combine(pos_flat, x1, gate, ga2, g_final, y, tm=128):
    b, l, dm = x1.shape
    per = l // tm
    tok = lambda width: pl.BlockSpec((1, tm, width), lambda i, pos: (i // per, i % per, 0))
    return pl.pallas_call(
        functools.partial(_combine_body, tm=tm),
        grid_spec=pltpu.PrefetchScalarGridSpec(
            num_scalar_prefetch=1,
            grid=(b * per,),
            in_specs=[tok(dm), tok(LANES),
                      pl.BlockSpec((1, 1, dm), lambda i, pos: (i // per, 0, 0)),
                      pl.BlockSpec((1, dm), lambda i, pos: (0, 0)),
                      pl.BlockSpec(memory_space=pl.ANY)],
            out_specs=tok(dm),
            scratch_shapes=[pltpu.VMEM((2, TOP_K, tm) + y.shape[1:], y.dtype), pltpu.SemaphoreType.DMA((2,))]),
        out_shape=jax.ShapeDtypeStruct((b, l, dm), F32),
        compiler_params=_cparams(("arbitrary",), 48),
        name="moe_combine",
    )(pos_flat, x1, gate, ga2, g_final, y)


def _filter_features(length):
    t = jnp.linspace(0.0, 1.0, length, dtype=F32)[:, None]
    bands = jnp.linspace(1e-4, HY_BANDS - 1, HY_BANDS, dtype=F32)
    ang = (2 * math.pi / length) * jnp.arange(length, dtype=F32)[:, None] * bands
    z = jnp.concatenate([t, jnp.cos(ang), -jnp.sin(ang)], axis=-1)
    return jnp.pad(z, ((0, 0), (0, LANES - z.shape[1])))


def _segments(counts):
    padded = (counts + ROW_BLOCK - 1) // ROW_BLOCK * ROW_BLOCK
    seg_end = jnp.cumsum(padded)
    seg_start = seg_end - padded
    total = (seg_end[-1:] // ROW_BLOCK).astype(I32)
    return (seg_start, (seg_start // ROW_BLOCK).astype(I32), (padded // ROW_BLOCK).astype(I32), total,
            seg_start + counts, padded - counts)


def kernel(x, c, ctx, c_ctx, w_ada, b_ada, g_norm1, g_norm2, w_in, s5_lam_re, s5_lam_im, s5_log_step,
           s5_b_re, s5_b_im, s5_c_re, s5_c_im, s5_d, s5_w_glu, s5_b_glu, hy_conv_w, hy_conv_b,
           hy_w1, hy_b1, hy_w2, hy_b2, hy_w3, hy_b3, hy_w4, hy_freq, hy_deltas, hy_d, w_out,
           w_router, b_router, w_gate_up, b_gate_up, w_down, b_down, g_final):
    assert w_ada.shape[0] == 1, "single-layer block"
    b, l, dm = x.shape
    lc = ctx.shape[1]
    s5w = s5_d.shape[1]
    hyw = hy_d.shape[1]
    n_exp = w_router.shape[2]
    row = lambda a: a.reshape(1, -1)

    cc = jnp.concatenate([c, c_ctx[None], jnp.zeros((2 * SUBLANES - b - 1, dm), F32)], axis=0)
    mod = _adaln(cc, w_ada[0], row(b_ada[0]))
    sh1, sc1, ga1, sh2, sc2, ga2 = [mod[:b, k * dm:(k + 1) * dm].reshape(b, 1, dm) for k in range(N_MOD)]
    csh1, csc1 = [mod[b:b + 1, k * dm:(k + 1) * dm].reshape(1, 1, dm) for k in range(2)]

    w_in_b = w_in[0].astype(BF16)
    g1 = row(g_norm1[0])
    p = _inproj(x, g1, sh1, sc1, w_in_b, s5w, tm=512)
    p_ctx = _inproj(ctx, g1, csh1, csc1, w_in_b, s5w, tm=lc)
    v, x0 = _inproj_hyena(x, g1, sh1, sc1, w_in_b, hy_conv_w[0], row(hy_conv_b[0]), hyw, s5w)

    bw, cw, lamr, lami = _s5_weights(s5_lam_re[0], s5_lam_im[0], s5_log_step[0],
                                     s5_b_re[0], s5_b_im[0], s5_c_re[0], s5_c_im[0])
    zero_state = jnp.zeros((2, bw.shape[1], SUBLANES, 2 * MODES_PER_BLOCK), F32)
    (ctx_state,) = _s5_scan(p_ctx, bw, cw, lamr, lami, zero_state, tl=64, emit_y=False)
    y_s5, _ = _s5_scan(p, bw, cw, lamr, lami, ctx_state, tl=64, emit_y=True)

    ka, kd = _hyena_filter(_filter_features(l), jnp.pad(hy_w1[0], ((0, LANES - hy_w1.shape[1]), (0, 0))),
                           row(hy_b1[0]), hy_w2[0], row(hy_b2[0]), hy_w3[0], row(hy_b3[0]),
                           row(hy_freq[0]), hy_w4[0], row(hy_deltas[0]))
    ctab, stab = _dft_tables(l)
    kk, kn = _hyena_spectrum(ctab, stab, ka, kd)
    y_hy = _hyena_conv(v, x0, kk, kn, row(hy_d[0]), ctab, stab)

    w_rh, w_rl = _split_bf16(jnp.pad(w_router[0], ((0, 0), (0, LANES - n_exp))))
    b_r = jnp.concatenate([b_router[0], jnp.full((LANES - n_exp,), NEG_BIG, F32)]).reshape(1, LANES)
    x1, h2, te, gate, cnt = _outproj(y_s5, p, y_hy, x, row(s5_d[0]), s5_w_glu[0].astype(BF16),
                                     row(s5_b_glu[0]), w_out[0].astype(BF16), ga1, row(g_norm2[0]),
                                     sh2, sc2, w_rh, w_rl, b_r)

    t = b * l
    n_blocks = t * TOP_K // ROW_BLOCK + n_exp
    counts = cnt[0, :n_exp].astype(I32)
    seg_start, first_blk, n_blk, total, pad_start, pad_len = _segments(counts)
    seg_row = jnp.broadcast_to(jnp.pad(seg_start.astype(F32), (0, LANES - n_exp)), (SUBLANES, LANES))
    pos = _slots(te.reshape(t, LANES), seg_row)
    pos_flat = pos[:, :TOP_K].reshape(-1)
    xs = _dispatch(pos_flat, pad_start, pad_len, h2.reshape(t, dm // LANES, LANES), n_blocks * ROW_BLOCK)
    hid = _expert_gateup(first_blk, n_blk, total, xs, w_gate_up[0], b_gate_up[0].reshape(n_exp, 1, -1))
    y_e = _expert_down(first_blk, n_blk, total, hid, w_down[0], b_down[0].reshape(n_exp, 1, -1))
    return _combine(pos_flat, x1, gate, ga2, row(g_final), y_e)
```

```python
import functools
import math

import jax
import jax.numpy as jnp
from jax import lax
from jax.experimental import pallas as pl
from jax.experimental.pallas import tpu as pltpu

F32 = jnp.float32
BF16 = jnp.bfloat16
I32 = jnp.int32
U32 = jnp.uint32
HIGHEST = lax.Precision.HIGHEST

GRID_W = 64
S5_H = 16
S5_P = 64
S5_RE_MAX = -1e-4
HY_BANDS = 16
N_EXPERTS = 32
TOP_K = 4
SWIGLU_LIMIT = 7.0
SWIGLU_ALPHA = 1.702
N_MOD = 6
EPS = 1e-6

LANES = 128
SUBLANES = 8
GROUPS_PER_BLOCK = LANES // S5_H
MODES_PER_BLOCK = GROUPS_PER_BLOCK * S5_P
ROW_BLOCK = 256
IN_SLOTS = 3
NEG_BIG = -1e30


def _cparams(sem, vmem_mb):
    return pltpu.CompilerParams(dimension_semantics=sem, vmem_limit_bytes=vmem_mb << 20)


def _dot(a, b):
    return jnp.dot(a, b, preferred_element_type=F32)


def _dot_hi(a, b):
    return jnp.dot(a, b, preferred_element_type=F32, precision=HIGHEST)


def _adaln_body(c_ref, w_ref, b_ref, o_ref):
    c = c_ref[...]
    s = c * jax.nn.sigmoid(c)
    o_ref[...] = _dot_hi(s, w_ref[...]) + b_ref[...]


def _adaln(cc, w, b, tn=1024):
    rows, d = cc.shape
    n = w.shape[1]
    return pl.pallas_call(
        _adaln_body,
        grid=(n // tn,),
        in_specs=[pl.BlockSpec((rows, d), lambda j: (0, 0)),
                  pl.BlockSpec((d, tn), lambda j: (0, j)),
                  pl.BlockSpec((1, tn), lambda j: (0, j))],
        out_specs=pl.BlockSpec((rows, tn), lambda j: (0, j)),
        out_shape=jax.ShapeDtypeStruct((rows, n), F32),
        compiler_params=_cparams(("arbitrary",), 40),
        name="adaln",
    )(cc, w, b)


def _norm_mod(x, g, sh, sc):
    ms = jnp.mean(x * x, axis=-1, keepdims=True)
    return (x * lax.rsqrt(ms + EPS) * g) * (1.0 + sc) + sh


def _inproj_body(x_ref, g_ref, sh_ref, sc_ref, w_ref, o_ref):
    half = x_ref.shape[1] // 2
    for r0 in (0, half):
        rs = slice(r0, r0 + half)
        h = _norm_mod(x_ref[0, rs, :], g_ref[...], sh_ref[0], sc_ref[0])
        o_ref[0, rs, :] = _dot(h.astype(BF16), w_ref[...])


def _inproj(x, g, sh, sc, w, n_cols, tm, tn=1024):
    b, l, d = x.shape
    if sh.shape[0] > 1:
        bidx = lambda n, bi, li: (bi, 0, 0)
    else:
        bidx = lambda n, bi, li: (0, 0, 0)
    return pl.pallas_call(
        _inproj_body,
        grid=(n_cols // tn, b, l // tm),
        in_specs=[pl.BlockSpec((1, tm, d), lambda n, bi, li: (bi, li, 0)),
                  pl.BlockSpec((1, d), lambda n, bi, li: (0, 0)),
                  pl.BlockSpec((1, 1, d), bidx),
                  pl.BlockSpec((1, 1, d), bidx),
                  pl.BlockSpec((d, tn), lambda n, bi, li: (0, n))],
        out_specs=pl.BlockSpec((1, tm, tn), lambda n, bi, li: (bi, li, n)),
        out_shape=jax.ShapeDtypeStruct((b, l, n_cols), F32),
        compiler_params=_cparams(("arbitrary", "arbitrary", "arbitrary"), 40),
        name="inproj",
    )(x, g, sh, sc, w)


def _s5prep_body(lre_ref, lim_ref, ls_ref, bre_ref, bim_ref, lbr_ref, lbi_ref, bbr_ref, bbi_ref):
    lr = jnp.minimum(lre_ref[...], S5_RE_MAX)
    li = lim_ref[...]
    dt = jnp.exp(ls_ref[...])
    er = jnp.exp(lr * dt)
    lbr = er * jnp.cos(li * dt)
    lbi = er * jnp.sin(li * dt)
    nr = lbr - 1.0
    den = lr * lr + li * li
    qr = (nr * lr + lbi * li) / den
    qi = (lbi * lr - nr * li) / den
    bre = bre_ref[...]
    bim = bim_ref[...]
    lbr_ref[...] = lbr
    lbi_ref[...] = lbi
    bbr_ref[...] = qr * bre - qi * bim
    bbi_ref[...] = qr * bim + qi * bre


def _s5_weights(lam_re, lam_im, log_step, b_re, b_im, c_re, c_im):
    nd, g, p = lam_re.shape
    h = S5_H
    nj = g // GROUPS_PER_BLOCK
    rep = lambda a: jnp.repeat(a.reshape(nd * g, 1, p), h, axis=1).reshape(nd * g * h, p)
    tr = lambda a: jnp.transpose(a, (0, 1, 3, 2)).reshape(nd * g * h, p)
    shp = jax.ShapeDtypeStruct((nd * g * h, p), F32)
    lbr, lbi, bbr, bbi = pl.pallas_call(
        _s5prep_body, out_shape=(shp, shp, shp, shp), name="s5prep",
    )(rep(lam_re), rep(lam_im), rep(log_step), tr(b_re), tr(b_im))
    eye = jnp.eye(GROUPS_PER_BLOCK, dtype=F32)
    bb = jnp.stack([bbr, bbi]).reshape(2, nd, nj, GROUPS_PER_BLOCK, h, p)
    bw = jnp.einsum('rdjahp,ab->djahrbp', bb, eye).reshape(nd, nj, LANES, 2 * MODES_PER_BLOCK)
    cc = jnp.stack([c_re, -c_im]).reshape(2, nd, nj, GROUPS_PER_BLOCK, h, p)
    cw = jnp.einsum('rdjahp,ab->djrapbh', cc, eye).reshape(nd, nj, 2 * MODES_PER_BLOCK, LANES)
    pick = lambda a: a.reshape(nd, g, h, p)[:, :, 0, :].reshape(nd, nj, 1, MODES_PER_BLOCK)
    return bw.astype(BF16), cw.astype(BF16), pick(lbr), pick(lbi)


def _s5_body(*refs, tl, nt, nj, emit_y):
    if emit_y:
        (u_hbm, bw_ref, cw_ref, lr_ref, li_ref, init_ref, y_hbm, fin_ref,
         u_scr, bu_scr, st_scr, in_sem, y_scr, out_sem) = refs
    else:
        (u_hbm, bw_ref, lr_ref, li_ref, init_ref, fin_ref, u_scr, bu_scr, st_scr, in_sem) = refs
    g = pl.program_id(0)
    ng = pl.num_programs(0)
    d = g // nt
    t = g % nt
    slot = g % 2
    nb = SUBLANES
    mpb = MODES_PER_BLOCK
    c = nj * LANES
    rows = tl * nb

    def tile_of(step):
        dd = step // nt
        tt = step % nt
        return dd, tt + dd * (nt - 1 - 2 * tt)

    def in_copies(step, sl):
        _, tile = tile_of(step)
        return [pltpu.make_async_copy(u_hbm.at[b, pl.ds(tile * tl, tl), pl.ds(0, c)],
                                      u_scr.at[sl, :, b, :], in_sem.at[sl]) for b in range(nb)]

    def out_copies(step, sl):
        dd, tile = tile_of(step)
        return [pltpu.make_async_copy(y_scr.at[sl, :, b, :], y_hbm.at[dd, b, pl.ds(tile * tl, tl), :],
                                      out_sem.at[sl]) for b in range(nb)]

    @pl.when(g == 0)
    def _():
        for cp in in_copies(0, 0):
            cp.start()

    @pl.when(g + 1 < ng)
    def _():
        for cp in in_copies(g + 1, (g + 1) % 2):
            cp.start()

    for cp in in_copies(g, slot):
        cp.wait()

    if emit_y:
        @pl.when(g >= 2)
        def _():
            for cp in out_copies(g - 2, slot):
                cp.wait()

    @pl.when(t == 0)
    def _():
        st_scr[...] = init_ref[0]

    def project(j):
        s = u_scr[slot, :, :, j * LANES:(j + 1) * LANES].reshape(rows, LANES)
        bu_scr[j % 3] = _dot(s.astype(BF16), bw_ref[0, j])

    def readout(j):
        y = _dot(bu_scr[j % 3].astype(BF16), cw_ref[0, j])
        y_scr[slot, :, :, j * LANES:(j + 1) * LANES] = y.reshape(tl, nb, LANES)

    project(0)
    for j in range(nj):
        if j + 1 < nj:
            project(j + 1)
        if emit_y and j >= 1:
            readout(j - 1)
        lr = jnp.broadcast_to(lr_ref[0, j], (nb, mpb))
        li = jnp.broadcast_to(li_ref[0, j], (nb, mpb))
        cur = j % 3
        xr = st_scr[j, :, 0:mpb]
        xi = st_scr[j, :, mpb:2 * mpb]
        for s in range(tl):
            r0 = pl.multiple_of((s + d * (tl - 1 - 2 * s)) * nb, nb)
            br = bu_scr[cur, pl.ds(r0, nb), 0:mpb]
            bi = bu_scr[cur, pl.ds(r0, nb), mpb:2 * mpb]
            xr, xi = lr * xr - li * xi + br, lr * xi + li * xr + bi
            bu_scr[cur, pl.ds(r0, nb), 0:mpb] = xr
            bu_scr[cur, pl.ds(r0, nb), mpb:2 * mpb] = xi
        st_scr[j, :, 0:mpb] = xr
        st_scr[j, :, mpb:2 * mpb] = xi

    if emit_y:
        readout(nj - 1)
        for cp in out_copies(g, slot):
            cp.start()

        @pl.when(g == ng - 1)
        def _():
            for cp in out_copies(g - 1, 1 - slot):
                cp.wait()
            for cp in out_copies(g, slot):
                cp.wait()

    @pl.when(t == nt - 1)
    def _():
        fin_ref[0] = st_scr[...]


def _s5_scan(u_arr, bw, cw, lamr, lami, init, tl, emit_y):
    b, l = u_arr.shape[:2]
    assert b == SUBLANES and l % tl == 0
    nj = bw.shape[1]
    c = nj * LANES
    nt = l // tl
    m2 = 2 * MODES_PER_BLOCK
    wmap = lambda g: (g // nt, 0, 0, 0)
    in_specs = [pl.BlockSpec(memory_space=pl.ANY), pl.BlockSpec((1, nj, LANES, m2), wmap)]
    args = [u_arr, bw]
    if emit_y:
        in_specs.append(pl.BlockSpec((1, nj, m2, LANES), wmap))
        args.append(cw)
    in_specs += [pl.BlockSpec((1, nj, 1, MODES_PER_BLOCK), wmap),
                 pl.BlockSpec((1, nj, 1, MODES_PER_BLOCK), wmap),
                 pl.BlockSpec((1, nj, SUBLANES, m2), wmap)]
    args += [lamr, lami, init]
    fin_spec = pl.BlockSpec((1, nj, SUBLANES, m2), wmap)
    fin_shape = jax.ShapeDtypeStruct((2, nj, SUBLANES, m2), F32)
    scratch = [pltpu.VMEM((2, tl, SUBLANES, c), F32),
               pltpu.VMEM((3, tl * SUBLANES, m2), F32),
               pltpu.VMEM((nj, SUBLANES, m2), F32),
               pltpu.SemaphoreType.DMA((2,))]
    if emit_y:
        out_specs = [pl.BlockSpec(memory_space=pl.ANY), fin_spec]
        out_shape = [jax.ShapeDtypeStruct((2, b, l, c), F32), fin_shape]
        scratch += [pltpu.VMEM((2, tl, SUBLANES, c), F32), pltpu.SemaphoreType.DMA((2,))]
    else:
        out_specs = [fin_spec]
        out_shape = [fin_shape]
    return pl.pallas_call(
        functools.partial(_s5_body, tl=tl, nt=nt, nj=nj, emit_y=emit_y),
        grid=(2 * nt,),
        in_specs=in_specs, out_specs=out_specs, out_shape=out_shape,
        scratch_shapes=scratch,
        compiler_params=_cparams(("arbitrary",), 48),
        name="s5scan_lat" if emit_y else "s5scan_ctx",
    )(*args)


def _hyfilt_body(z_ref, w1_ref, b1_ref, w2_ref, b2_ref, w3_ref, b3_ref, fr_ref,
                 w4f_ref, w4b_ref, df_ref, db_ref, ka_ref, kd_ref, h_scr, *, length):
    z = z_ref[...]

    @pl.when(pl.program_id(0) == 0)
    def _():
        fr = fr_ref[...]
        h = jnp.sin(fr * (_dot_hi(z, w1_ref[...]) + b1_ref[...]))
        h = jnp.sin(fr * (_dot_hi(h, w2_ref[...]) + b2_ref[...]))
        h_scr[...] = jnp.sin(fr * (_dot_hi(h, w3_ref[...]) + b3_ref[...]))

    h = h_scr[...]
    t = z[:, 0:1]
    hf = _dot_hi(h, w4f_ref[...]) * jnp.exp(-t * jnp.abs(df_ref[...]))
    hb = _dot_hi(h, w4b_ref[...]) * jnp.exp(-t * jnp.abs(db_ref[...]))
    row = lax.broadcasted_iota(I32, hf.shape, 0)
    hb = jnp.where(row < length - 1, hb, 0.0)
    norm = jnp.sum(jnp.abs(hf), axis=0, keepdims=True) + jnp.sum(jnp.abs(hb), axis=0, keepdims=True)
    kf = hf / norm
    kb = jnp.where(row >= 1, pltpu.roll(hb, 1, axis=0), 0.0) / norm
    ka_ref[...] = kf + kb
    kd_ref[...] = kf - kb


def _hyena_filter(z, w1, b1, w2, b2, w3, b3, fr, w4, deltas, ct=256):
    length, ze = z.shape
    order = w2.shape[0]
    c = w4.shape[1] // 2
    nct = c // ct
    full = lambda shape: pl.BlockSpec(shape, lambda i: (0, 0))
    shp = jax.ShapeDtypeStruct((length, c), F32)
    return pl.pallas_call(
        functools.partial(_hyfilt_body, length=length),
        grid=(nct,),
        in_specs=[full((length, ze)), full((ze, order)), full((1, order)),
                  full((order, order)), full((1, order)), full((order, order)), full((1, order)),
                  full((1, order)),
                  pl.BlockSpec((order, ct), lambda i: (0, i)),
                  pl.BlockSpec((order, ct), lambda i: (0, nct + i)),
                  pl.BlockSpec((1, ct), lambda i: (0, i)),
                  pl.BlockSpec((1, ct), lambda i: (0, nct + i))],
        out_specs=[pl.BlockSpec((length, ct), lambda i: (0, i))] * 2,
        out_shape=[shp, shp],
        scratch_shapes=[pltpu.VMEM((length, order), F32)],
        compiler_params=_cparams(("arbitrary",), 48),
        name="hyena_filter",
    )(z, w1, b1, w2, b2, w3, b3, fr, w4, w4, deltas, deltas)


def _dfttab_body(ca_ref, sa_ref, cb_ref, sb_ref, c_ref, s_ref):
    ca = ca_ref[0]
    sa = sa_ref[0]
    cb = cb_ref[...]
    sb = sb_ref[...]
    c_ref[...] = (ca * cb - sa * sb).astype(c_ref.dtype)
    s_ref[...] = (sa * cb + ca * sb).astype(s_ref.dtype)


def _dft_tables(length, r=128):
    n = 2 * length
    s = jnp.arange(length, dtype=I32)[None, :]
    ang = lambda k: (2.0 * math.pi / n) * ((k * s) % n).astype(F32)
    aa = ang(jnp.arange(length // r, dtype=I32)[:, None] * r)
    ab = ang(jnp.arange(r, dtype=I32)[:, None])
    rows3 = lambda a: a.reshape(length // r, 1, length)
    shp = jax.ShapeDtypeStruct((length, length), BF16)
    return pl.pallas_call(
        _dfttab_body,
        grid=(length // r,),
        in_specs=[pl.BlockSpec((None, 1, length), lambda i: (i, 0, 0)),
                  pl.BlockSpec((None, 1, length), lambda i: (i, 0, 0)),
                  pl.BlockSpec((r, length), lambda i: (0, 0)),
                  pl.BlockSpec((r, length), lambda i: (0, 0))],
        out_specs=[pl.BlockSpec((r, length), lambda i: (i, 0))] * 2,
        out_shape=[shp, shp],
        compiler_params=_cparams(("arbitrary",), 32),
        name="dft_tables",
    )(rows3(jnp.cos(aa)), rows3(jnp.sin(aa)), jnp.cos(ab), jnp.sin(ab))


def _hyspec_body(c_ref, s_ref, ka_ref, kd_ref, kk_ref, kn_ref, *, length, fq):
    q = pl.program_id(1)
    nq = length // fq
    n = 2.0 * length

    @pl.when(q < nq)
    def _():
        grow = q * fq + lax.broadcasted_iota(I32, (fq, 1), 0)
        scale = jnp.where(grow == 0, 1.0 / n, 2.0 / n)
        kk_ref[...] = _dot(c_ref[...], ka_ref[...].astype(BF16)) * scale

    @pl.when(q >= nq)
    def _():
        kk_ref[...] = _dot(s_ref[...], kd_ref[...].astype(BF16)) * (2.0 / n)

    @pl.when(q == 0)
    def _():
        a = ka_ref[...]
        row = lax.broadcasted_iota(I32, a.shape, 0)
        sgn = jnp.where(row % 2 == 0, 1.0, -1.0)
        kn = jnp.sum(a * sgn, axis=0, keepdims=True) * (1.0 / n)
        kn_ref[...] = jnp.broadcast_to(kn, kn_ref.shape)


def _hyena_spectrum(ctab, stab, ka, kd, ct=256, fq=512):
    length, c = ka.shape
    nq = length // fq
    return pl.pallas_call(
        functools.partial(_hyspec_body, length=length, fq=fq),
        grid=(c // ct, 2 * nq),
        in_specs=[pl.BlockSpec((fq, length), lambda i, q: (jnp.minimum(q, nq - 1), 0)),
                  pl.BlockSpec((fq, length), lambda i, q: (jnp.maximum(q - nq, 0), 0)),
                  pl.BlockSpec((length, ct), lambda i, q: (0, i)),
                  pl.BlockSpec((length, ct), lambda i, q: (0, i))],
        out_specs=[pl.BlockSpec((fq, ct), lambda i, q: (q, i)),
                   pl.BlockSpec((SUBLANES, ct), lambda i, q: (0, i))],
        out_shape=[jax.ShapeDtypeStruct((2 * length, c), F32), jax.ShapeDtypeStruct((SUBLANES, c), F32)],
        compiler_params=_cparams(("arbitrary", "arbitrary"), 48),
        name="hyena_spectrum",
    )(ctab, stab, ka, kd)


def _inproj_hy_body(x_ref, g_ref, sh_ref, sc_ref, wv_ref, w1_ref, w0_ref, cv_ref, c1_ref, c0_ref,
                    bv_ref, b1_ref, b0_ref, v_ref, x0_ref):
    half = x_ref.shape[1] // 2
    row = lax.broadcasted_iota(I32, (half, 1), 0) % GRID_W

    def sconv(a, w_ref, b_ref):
        prev = jnp.where(row == 0, 0.0, pltpu.roll(a, 1, axis=0))
        nxt = jnp.where(row == GRID_W - 1, 0.0, pltpu.roll(a, half - 1, axis=0))
        return prev * w_ref[0:1, :] + a * w_ref[1:2, :] + nxt * w_ref[2:3, :] + b_ref[...]

    for r0 in (0, half):
        rs = slice(r0, r0 + half)
        h = _norm_mod(x_ref[0, rs, :], g_ref[...], sh_ref[0], sc_ref[0]).astype(BF16)
        zv = sconv(_dot(h, wv_ref[...]), cv_ref, bv_ref)
        z1 = sconv(_dot(h, w1_ref[...]), c1_ref, b1_ref)
        v_ref[0, rs, :] = (zv * z1).astype(v_ref.dtype)
        x0_ref[0, rs, :] = sconv(_dot(h, w0_ref[...]), c0_ref, b0_ref).astype(x0_ref.dtype)


def _inproj_hyena(x, g, sh, sc, w, conv_w, conv_b, c, first_col, tm=512, ct=256):
    b, l, d = x.shape
    assert tm % (2 * GRID_W) == 0
    nct = c // ct
    base = first_col // ct
    tokmap = lambda ci, bi, li: (bi, li, 0)
    perb = pl.BlockSpec((1, 1, d), lambda ci, bi, li: (bi, 0, 0))
    wspec = lambda k: pl.BlockSpec((d, ct), lambda ci, bi, li: (0, base + k * nct + ci))
    cspec = lambda k: pl.BlockSpec((3, ct), lambda ci, bi, li: (0, k * nct + ci))
    bspec = lambda k: pl.BlockSpec((1, ct), lambda ci, bi, li: (0, k * nct + ci))
    shp = jax.ShapeDtypeStruct((b, l, c), BF16)
    return pl.pallas_call(
        _inproj_hy_body,
        grid=(nct, b, l // tm),
        in_specs=[pl.BlockSpec((1, tm, d), tokmap), pl.BlockSpec((1, d), lambda ci, bi, li: (0, 0)), perb, perb,
                  wspec(0), wspec(1), wspec(2), cspec(0), cspec(1), cspec(2), bspec(0), bspec(1), bspec(2)],
        out_specs=[pl.BlockSpec((1, tm, ct), lambda ci, bi, li: (bi, li, ci))] * 2,
        out_shape=[shp, shp],
        compiler_params=_cparams(("arbitrary", "arbitrary", "arbitrary"), 40),
        name="inproj_hyena",
    )(x, g, sh, sc, w, w, w, conv_w, conv_w, conv_w, conv_b, conv_b, conv_b)


def _hyconv_body(v_ref, x0_ref, kk_ref, kn_ref, d_ref, c_hbm, s_hbm, o_ref, c_scr, s_scr, sem, *, length, fq):
    first = jnp.logical_and(pl.program_id(0) == 0, pl.program_id(1) == 0)

    @pl.when(first)
    def _():
        cc = pltpu.make_async_copy(c_hbm, c_scr, sem.at[0])
        cs = pltpu.make_async_copy(s_hbm, s_scr, sem.at[1])
        cc.start()
        cs.start()
        cc.wait()
        cs.wait()

    vb = v_ref[0]
    v = vb.astype(F32)
    row = lax.broadcasted_iota(I32, (length, 1), 0)
    sgn = jnp.where(row % 2 == 0, 1.0, -1.0)
    vn = jnp.sum(v * sgn, axis=0, keepdims=True)
    acc = sgn * (vn * kn_ref[0:1, :])
    for q in range(length // fq):
        lo, hi = q * fq, (q + 1) * fq
        vr = _dot(c_scr[lo:hi, :], vb)
        vs = _dot(s_scr[lo:hi, :], vb)
        kr = kk_ref[lo:hi, :]
        ks = kk_ref[length + lo:length + hi, :]
        pr = vr * kr - vs * ks
        ps = vr * ks + vs * kr
        acc = acc + _dot(c_scr[:, lo:hi], pr.astype(BF16)) + _dot(s_scr[:, lo:hi], ps.astype(BF16))
    o_ref[0] = ((acc + v * d_ref[...]) * x0_ref[0].astype(F32)).astype(o_ref.dtype)


def _hyena_conv(v, x0, kk, kn, d_skip, ctab, stab, ct=256, fq=512):
    b, l, c = v.shape
    return pl.pallas_call(
        functools.partial(_hyconv_body, length=l, fq=fq),
        grid=(c // ct, b),
        in_specs=[pl.BlockSpec((1, l, ct), lambda ci, bi: (bi, 0, ci)),
                  pl.BlockSpec((1, l, ct), lambda ci, bi: (bi, 0, ci)),
                  pl.BlockSpec((2 * l, ct), lambda ci, bi: (0, ci)),
                  pl.BlockSpec((SUBLANES, ct), lambda ci, bi: (0, ci)),
                  pl.BlockSpec((1, ct), lambda ci, bi: (0, ci)),
                  pl.BlockSpec(memory_space=pl.ANY),
                  pl.BlockSpec(memory_space=pl.ANY)],
        out_specs=pl.BlockSpec((1, l, ct), lambda ci, bi: (bi, 0, ci)),
        out_shape=jax.ShapeDtypeStruct((b, l, c), BF16),
        scratch_shapes=[pltpu.VMEM((l, l), BF16), pltpu.VMEM((l, l), BF16),
                        pltpu.SemaphoreType.DMA((2,))],
        compiler_params=_cparams(("arbitrary", "arbitrary"), 48),
        name="hyena_conv",
    )(v, x0, kk, kn, d_skip, ctab, stab)


def _split_bf16(a):
    hi = a.astype(BF16)
    return hi, (a - hi.astype(F32)).astype(BF16)


def _outproj_body(y_ref, u_ref, yh_ref, x_ref, d_ref, wg_ref, bg_ref, wo_ref, ga_ref, g2_ref,
                  sh_ref, sc_ref, wrh_ref, wrl_ref, br_ref, x1_ref, hp_ref, te_ref, gt_ref, cnt_ref, *, sub):
    tm, c = u_ref.shape[1], u_ref.shape[2]
    dm = x_ref.shape[2]

    @pl.when(jnp.logical_and(pl.program_id(0) == 0, pl.program_id(1) == 0))
    def _():
        cnt_ref[...] = jnp.zeros_like(cnt_ref)

    lane = lax.broadcasted_iota(I32, (sub, LANES), 1)
    total = jnp.zeros((1, LANES), F32)
    for r0 in range(0, tm, sub):
        rs = slice(r0, r0 + sub)
        ys = y_ref[0, 0, rs, :] + y_ref[1, 0, rs, :] + u_ref[0, rs, :] * d_ref[...]
        ge = 0.5 * ys * (1.0 + lax.erf(ys * (1.0 / math.sqrt(2.0))))
        gl = ge * jax.nn.sigmoid(_dot(ge.astype(BF16), wg_ref[...]) + bg_ref[...])
        mix = _dot(gl.astype(BF16), wo_ref[0:c, :]) + _dot(yh_ref[0, rs, :], wo_ref[c:2 * c, :])
        x1 = x_ref[0, rs, :] + ga_ref[0] * mix
        x1_ref[0, rs, :] = x1
        h2 = _norm_mod(x1, g2_ref[...], sh_ref[0], sc_ref[0])
        hp_ref[0, rs] = h2.astype(BF16).reshape(sub, dm // LANES, LANES)

        hh, hl = _split_bf16(h2)
        logits = _dot(hh, wrh_ref[...]) + (_dot(hh, wrl_ref[...]) + _dot(hl, wrh_ref[...])) + br_ref[...]
        work = logits
        vals, idxs = [], []
        for _ in range(TOP_K):
            m = jnp.max(work, axis=-1, keepdims=True)
            ix = jnp.min(jnp.where(work == m, lane, LANES), axis=-1, keepdims=True)
            vals.append(m)
            idxs.append(ix)
            work = jnp.where(lane == ix, NEG_BIG, work)
        ex = [jnp.exp(v - vals[0]) for v in vals]
        den = ex[0]
        for e in ex[1:]:
            den = den + e
        te = jnp.zeros((sub, LANES), I32)
        gt = jnp.zeros((sub, LANES), F32)
        mh = jnp.zeros((sub, LANES), F32)
        for k in range(TOP_K):
            te = jnp.where(lane == k, idxs[k], te)
            gt = jnp.where(lane == k, ex[k] / den, gt)
            mh = mh + (lane == idxs[k]).astype(F32)
        te_ref[0, rs, :] = te
        gt_ref[0, rs, :] = gt
        total = total + jnp.sum(mh, axis=0, keepdims=True)

    cnt_ref[...] += jnp.broadcast_to(total, cnt_ref.shape)


def _outproj(y, p, yh, x, d_skip, w_glu, b_glu, w_out, ga1, g2, sh2, sc2, w_rh, w_rl, b_r, tm=256, sub=128):
    b, l, dm = x.shape
    c = yh.shape[2]
    tok = lambda width: pl.BlockSpec((1, tm, width), lambda bi, li: (bi, li, 0))
    full = lambda shape: pl.BlockSpec(shape, lambda bi, li: (0,) * len(shape))
    perb = pl.BlockSpec((1, 1, dm), lambda bi, li: (bi, 0, 0))
    return pl.pallas_call(
        functools.partial(_outproj_body, sub=sub),
        grid=(b, l // tm),
        in_specs=[pl.BlockSpec((2, 1, tm, c), lambda bi, li: (0, bi, li, 0)),
                  tok(c), tok(c), tok(dm),
                  full((1, c)), full((c, c)), full((1, c)), full((2 * c, dm)),
                  perb, full((1, dm)), perb, perb,
                  full((dm, LANES)), full((dm, LANES)), full((1, LANES))],
        out_specs=[tok(dm), pl.BlockSpec((1, tm, dm // LANES, LANES), lambda bi, li: (bi, li, 0, 0)),
                   tok(LANES), tok(LANES), full((SUBLANES, LANES))],
        out_shape=[jax.ShapeDtypeStruct((b, l, dm), F32),
                   jax.ShapeDtypeStruct((b, l, dm // LANES, LANES), BF16),
                   jax.ShapeDtypeStruct((b, l, LANES), I32),
                   jax.ShapeDtypeStruct((b, l, LANES), F32),
                   jax.ShapeDtypeStruct((SUBLANES, LANES), F32)],
        compiler_params=_cparams(("arbitrary", "arbitrary"), 56),
        name="outproj_router",
    )(y, p, yh, x, d_skip, w_glu, b_glu, w_out, ga1, g2, sh2, sc2, w_rh, w_rl, b_r)


def _slots_body(te_ref, seg_ref, pos_ref, carry):
    @pl.when(pl.program_id(0) == 0)
    def _():
        carry[...] = jnp.zeros_like(carry)

    te = te_ref[...]
    tm = te.shape[0]
    lane = lax.broadcasted_iota(I32, (tm, LANES), 1)
    sel = [lane == te[:, k:k + 1] for k in range(TOP_K)]
    mh = jnp.zeros((tm, LANES), F32)
    for s in sel:
        mh = mh + s.astype(F32)
    r = lax.broadcasted_iota(I32, (tm, tm), 0)
    cc = lax.broadcasted_iota(I32, (tm, tm), 1)
    strict_lower = (r > cc).astype(BF16)
    slot = _dot(strict_lower, mh.astype(BF16)) + carry[0:1, :] + seg_ref[0:1, :]
    pos = jnp.zeros((tm, LANES), F32)
    for k in range(TOP_K):
        pk = jnp.sum(jnp.where(sel[k], slot, 0.0), axis=-1, keepdims=True)
        pos = jnp.where(lane == k, pk, pos)
    pos_ref[...] = pos.astype(I32)
    carry[...] += jnp.broadcast_to(jnp.sum(mh, axis=0, keepdims=True), carry.shape)


def _slots(te, seg_start, tm=512):
    t = te.shape[0]
    return pl.pallas_call(
        _slots_body,
        grid=(t // tm,),
        in_specs=[pl.BlockSpec((tm, LANES), lambda i: (i, 0)),
                  pl.BlockSpec((SUBLANES, LANES), lambda i: (0, 0))],
        out_specs=pl.BlockSpec((tm, LANES), lambda i: (i, 0)),
        out_shape=jax.ShapeDtypeStruct((t, LANES), I32),
        scratch_shapes=[pltpu.VMEM((SUBLANES, LANES), F32)],
        compiler_params=_cparams(("arbitrary",), 32),
        name="moe_slots",
    )(te, seg_start)


PAD_CHUNKS = tuple(ROW_BLOCK >> k for k in range(1, ROW_BLOCK.bit_length()))


def _row_copy(src, src_row, dst, dst_row, sem):
    return pltpu.make_async_copy(src.at[pl.ds(src_row, 1)], dst.at[pl.ds(dst_row, 1)], sem)


def _dispatch_body(pos_ref, ps_ref, pn_ref, h_ref, xs_out, zbuf, sem, zsem, *, tm, n_exp, n_blocks):
    @pl.when(pl.program_id(0) == 0)
    def _():
        zbuf[...] = jnp.zeros_like(zbuf)

        def copies(e):
            start = ps_ref[e]
            length = pn_ref[e]
            out = []
            for size in PAD_CHUNKS:
                done = length & (-2 * size)
                dst = xs_out.at[pl.ds(start + done, size)]
                out.append(((length & size) != 0, pltpu.make_async_copy(zbuf.at[pl.ds(0, size)], dst, zsem)))
            return out

        def fill(e, carry):
            for on, cp in copies(e):
                pl.when(on)(cp.start)
            return carry

        def drain(e, carry):
            for on, cp in copies(e):
                pl.when(on)(cp.wait)
            return carry

        lax.fori_loop(0, n_exp, fill, 0)
        lax.fori_loop(0, n_exp, drain, 0)

        def tail(blk):
            rows = [blk * ROW_BLOCK + part * PAD_CHUNKS[0] for part in range(ROW_BLOCK // PAD_CHUNKS[0])]
            return [pltpu.make_async_copy(zbuf, xs_out.at[pl.ds(r, PAD_CHUNKS[0])], zsem) for r in rows]

        def fill_tail(blk, carry):
            for cp in tail(blk):
                cp.start()
            return carry

        def drain_tail(blk, carry):
            for cp in tail(blk):
                cp.wait()
            return carry

        first_unused = (ps_ref[n_exp - 1] + pn_ref[n_exp - 1]) // ROW_BLOCK
        lax.fori_loop(first_unused, n_blocks, fill_tail, 0)
        lax.fori_loop(first_unused, n_blocks, drain_tail, 0)

    base = pl.program_id(0) * (tm * TOP_K)

    def issue(r, carry):
        for k in range(TOP_K):
            _row_copy(h_ref, r, xs_out, pos_ref[base + r * TOP_K + k], sem).start(priority=k % 2)
        return carry

    lax.fori_loop(0, tm, issue, 0, unroll=8)

    def drain_rows(r, carry):
        for k in range(TOP_K):
            _row_copy(h_ref, 0, xs_out, 0, sem).wait()
        return carry

    lax.fori_loop(0, tm, drain_rows, 0, unroll=8)


def _dispatch(pos_flat, pad_start, pad_len, hp, n_rows, tm=256):
    t = hp.shape[0]
    tile = hp.shape[1:]
    return pl.pallas_call(
        functools.partial(_dispatch_body, tm=tm, n_exp=pad_start.shape[0], n_blocks=n_rows // ROW_BLOCK),
        grid_spec=pltpu.PrefetchScalarGridSpec(
            num_scalar_prefetch=3,
            grid=(t // tm,),
            in_specs=[pl.BlockSpec((tm,) + tile, lambda i, pos, ps, pn: (i, 0, 0))],
            out_specs=pl.BlockSpec(memory_space=pl.ANY),
            scratch_shapes=[pltpu.VMEM((PAD_CHUNKS[0],) + tile, hp.dtype),
                            pltpu.SemaphoreType.DMA(()), pltpu.SemaphoreType.DMA(())]),
        out_shape=jax.ShapeDtypeStruct((n_rows,) + tile, hp.dtype),
        compiler_params=_cparams(("arbitrary",), 32),
        name="moe_dispatch",
    )(pos_flat, pad_start, pad_len, hp)


def _stream_expert_blocks(first, count, total, n_blocks, src_hbm, dst_hbm, col0, ibuf, obuf, isem, osem, compute):
    e = pl.program_id(1)
    rb = ROW_BLOCK
    n_in = ibuf.shape[0]

    def rows(gs):
        return pl.ds(pl.multiple_of(gs * rb, rb), rb)

    def in_copy(gs):
        slot = gs % n_in
        return pltpu.make_async_copy(src_hbm.at[rows(gs)], ibuf.at[slot], isem.at[slot])

    def out_copy(gs, slot):
        dst = dst_hbm.at[rows(gs)] if col0 is None else dst_hbm.at[rows(gs), pl.ds(col0, obuf.shape[2])]
        return pltpu.make_async_copy(obuf.at[slot], dst, osem.at[slot])

    @pl.when(e == 0)
    def _():
        for gs in range(n_in - 1):
            pl.when(gs < total)(in_copy(gs).start)

    def block(s, carry):
        gs = first + s
        slot = gs % 2

        @pl.when(gs + n_in - 1 < total)
        def _():
            in_copy(gs + n_in - 1).start()

        in_copy(gs).wait()

        @pl.when(gs >= 2)
        def _():
            out_copy(gs - 2, slot).wait()

        compute(ibuf.at[gs % n_in], obuf.at[slot])
        out_copy(gs, slot).start()
        return carry

    lax.fori_loop(0, count, block, 0)

    @pl.when(e == pl.num_programs(1) - 1)
    def _():
        @pl.when(total >= 2)
        def _():
            out_copy(total - 2, total % 2).wait()

        out_copy(total - 1, (total - 1) % 2).wait()
        obuf[0] = jnp.zeros(obuf.shape[1:], obuf.dtype)

        def fill(gs, carry):
            out_copy(gs, 0).start()
            return carry

        def drain(gs, carry):
            out_copy(gs, 0).wait()
            return carry

        lax.fori_loop(total, n_blocks, fill, 0)
        lax.fori_loop(total, n_blocks, drain, 0)


def _gateup_body(fs_ref, ns_ref, tot_ref, xs_hbm, wg_ref, wu_ref, bg_ref, bu_ref, h_hbm,
                 wg_s, wu_s, xbuf, obuf, xsem, osem, *, n_blocks):
    e = pl.program_id(1)
    tn = obuf.shape[2]
    half = tn // 2

    @pl.when(ns_ref[e] > 0)
    def _():
        wg_s[...] = wg_ref[...].astype(BF16)
        wu_s[...] = wu_ref[...].astype(BF16)

    def compute(x_ref, o_ref):
        x = x_ref[...].reshape(ROW_BLOCK, wg_s.shape[0])
        for c0 in (0, half):
            cs = slice(c0, c0 + half)
            g = _dot(x, wg_s[:, cs]) + bg_ref[:, cs]
            u = _dot(x, wu_s[:, cs]) + bu_ref[:, cs]
            g = jnp.minimum(g, SWIGLU_LIMIT)
            u = jnp.clip(u, -SWIGLU_LIMIT, SWIGLU_LIMIT)
            o_ref[:, cs] = ((u + 1.0) * (g * jax.nn.sigmoid(SWIGLU_ALPHA * g))).astype(o_ref.dtype)

    col0 = pl.multiple_of(pl.program_id(0) * tn, tn)
    _stream_expert_blocks(fs_ref[e], ns_ref[e], tot_ref[0], n_blocks, xs_hbm, h_hbm, col0,
                          xbuf, obuf, xsem, osem, compute)


def _expert_gateup(first_blk, n_blk, total, xs, w_gate_up, b_gate_up, tn=1024):
    n_exp, dm, ff2 = w_gate_up.shape
    ff = ff2 // 2
    nr = xs.shape[0]
    nn = ff // tn
    return pl.pallas_call(
        functools.partial(_gateup_body, n_blocks=nr // ROW_BLOCK),
        grid_spec=pltpu.PrefetchScalarGridSpec(
            num_scalar_prefetch=3,
            grid=(nn, n_exp),
            in_specs=[pl.BlockSpec(memory_space=pl.ANY),
                      pl.BlockSpec((None, dm, tn), lambda j, e, fs, ns, tot: (e, 0, j)),
                      pl.BlockSpec((None, dm, tn), lambda j, e, fs, ns, tot: (e, 0, nn + j)),
                      pl.BlockSpec((None, 1, tn), lambda j, e, fs, ns, tot: (e, 0, j)),
                      pl.BlockSpec((None, 1, tn), lambda j, e, fs, ns, tot: (e, 0, nn + j))],
            out_specs=pl.BlockSpec(memory_space=pl.ANY),
            scratch_shapes=[pltpu.VMEM((dm, tn), BF16), pltpu.VMEM((dm, tn), BF16),
                            pltpu.VMEM((IN_SLOTS, ROW_BLOCK) + xs.shape[1:], xs.dtype),
                            pltpu.VMEM((2, ROW_BLOCK, tn), BF16),
                            pltpu.SemaphoreType.DMA((IN_SLOTS,)), pltpu.SemaphoreType.DMA((2,))]),
        out_shape=jax.ShapeDtypeStruct((nr, ff), BF16),
        compiler_params=_cparams(("arbitrary", "arbitrary"), 56),
        name="moe_gateup",
    )(first_blk, n_blk, total, xs, w_gate_up, w_gate_up, b_gate_up, b_gate_up)


def _down_body(fs_ref, ns_ref, tot_ref, h_hbm, w_ref, b_ref, y_hbm, w_s, hbuf, obuf, hsem, osem, *, n_blocks):
    e = pl.program_id(1)

    @pl.when(ns_ref[e] > 0)
    def _():
        w_s[...] = w_ref[...].astype(BF16)

    def compute(h_ref, o_ref):
        y = _dot(h_ref[...], w_s[...]) + b_ref[...]
        o_ref[...] = y.astype(o_ref.dtype).reshape(o_ref.shape)

    _stream_expert_blocks(fs_ref[e], ns_ref[e], tot_ref[0], n_blocks, h_hbm, y_hbm, None,
                          hbuf, obuf, hsem, osem, compute)


def _expert_down(first_blk, n_blk, total, h, w_down, b_down):
    n_exp, ff, dm = w_down.shape
    nr = h.shape[0]
    tile = (dm // LANES, LANES)
    return pl.pallas_call(
        functools.partial(_down_body, n_blocks=nr // ROW_BLOCK),
        grid_spec=pltpu.PrefetchScalarGridSpec(
            num_scalar_prefetch=3,
            grid=(1, n_exp),
            in_specs=[pl.BlockSpec(memory_space=pl.ANY),
                      pl.BlockSpec((None, ff, dm), lambda j, e, fs, ns, tot: (e, 0, 0)),
                      pl.BlockSpec((None, 1, dm), lambda j, e, fs, ns, tot: (e, 0, 0))],
            out_specs=pl.BlockSpec(memory_space=pl.ANY),
            scratch_shapes=[pltpu.VMEM((ff, dm), BF16),
                            pltpu.VMEM((IN_SLOTS, ROW_BLOCK, ff), h.dtype), pltpu.VMEM((2, ROW_BLOCK) + tile, BF16),
                            pltpu.SemaphoreType.DMA((IN_SLOTS,)), pltpu.SemaphoreType.DMA((2,))]),
        out_shape=jax.ShapeDtypeStruct((nr,) + tile, BF16),
        compiler_params=_cparams(("arbitrary", "arbitrary"), 56),
        name="moe_down",
    )(first_blk, n_blk, total, h, w_down, b_down)


def _combine_body(pos_ref, x1_ref, gt_ref, ga_ref, gf_ref, y_hbm, o_ref, ybuf, sem, *, tm):
    i = pl.program_id(0)
    n = pl.num_programs(0)

    def gather(tile, slot):
        def issue(r, carry):
            for k in range(TOP_K):
                p = pos_ref[(tile * tm + r) * TOP_K + k]
                _row_copy(y_hbm, p, ybuf.at[slot, k], r, sem.at[slot]).start(priority=k % 2)
            return carry
        lax.fori_loop(0, tm, issue, 0, unroll=8)

    @pl.when(i == 0)
    def _():
        gather(0, 0)

    @pl.when(i + 1 < n)
    def _():
        gather(i + 1, (i + 1) % 2)

    slot = i % 2

    def drain(r, carry):
        for k in range(TOP_K):
            _row_copy(y_hbm, 0, ybuf.at[slot, k], 0, sem.at[slot]).wait()
        return carry

    lax.fori_loop(0, tm, drain, 0, unroll=8)

    gt = gt_ref[0]
    dm = x1_ref.shape[2]
    rows_of = lambda k: ybuf[slot, k].reshape(tm, dm).astype(F32)
    acc = gt[:, 0:1] * rows_of(0)
    for k in range(1, TOP_K):
        acc = acc + gt[:, k:k + 1] * rows_of(k)
    x2 = x1_ref[0] + ga_ref[0] * acc
    ms = jnp.mean(x2 * x2, axis=-1, keepdims=True)
    o_ref[0] = x2 * lax.rsqrt(ms + EPS) * gf_ref[...]


def _combine(pos_flat, x1, gate, ga2, g_final, y, tm=128):
    b, l, dm = x1.shape
    per = l // tm
    tok = lambda width: pl.BlockSpec((1, tm, width), lambda i, pos: (i // per, i % per, 0))
    return pl.pallas_call(
        functools.partial(_combine_body, tm=tm),
        grid_spec=pltpu.PrefetchScalarGridSpec(
            num_scalar_prefetch=1,
            grid=(b * per,),
            in_specs=[tok(dm), tok(LANES),
                      pl.BlockSpec((1, 1, dm), lambda i, pos: (i // per, 0, 0)),
                      pl.BlockSpec((1, dm), lambda i, pos: (0, 0)),
                      pl.BlockSpec(memory_space=pl.ANY)],
            out_specs=tok(dm),
            scratch_shapes=[pltpu.VMEM((2, TOP_K, tm) + y.shape[1:], y.dtype), pltpu.SemaphoreType.DMA((2,))]),
        out_shape=jax.ShapeDtypeStruct((b, l, dm), F32),
        compiler_params=_cparams(("arbitrary",), 48),
        name="moe_combine",
    )(pos_flat, x1, gate, ga2, g_final, y)


def _filter_features(length):
    t = jnp.linspace(0.0, 1.0, length, dtype=F32)[:, None]
    bands = jnp.linspace(1e-4, HY_BANDS - 1, HY_BANDS, dtype=F32)
    ang = (2 * math.pi / length) * jnp.arange(length, dtype=F32)[:, None] * bands
    z = jnp.concatenate([t, jnp.cos(ang), -jnp.sin(ang)], axis=-1)
    return jnp.pad(z, ((0, 0), (0, LANES - z.shape[1])))


def _segments(counts):
    padded = (counts + ROW_BLOCK - 1) // ROW_BLOCK * ROW_BLOCK
    seg_end = jnp.cumsum(padded)
    seg_start = seg_end - padded
    total = (seg_end[-1:] // ROW_BLOCK).astype(I32)
    return (seg_start, (seg_start // ROW_BLOCK).astype(I32), (padded // ROW_BLOCK).astype(I32), total,
            seg_start + counts, padded - counts)


def kernel(x, c, ctx, c_ctx, w_ada, b_ada, g_norm1, g_norm2, w_in, s5_lam_re, s5_lam_im, s5_log_step,
           s5_b_re, s5_b_im, s5_c_re, s5_c_im, s5_d, s5_w_glu, s5_b_glu, hy_conv_w, hy_conv_b,
           hy_w1, hy_b1, hy_w2, hy_b2, hy_w3, hy_b3, hy_w4, hy_freq, hy_deltas, hy_d, w_out,
           w_router, b_router, w_gate_up, b_gate_up, w_down, b_down, g_final):
    assert w_ada.shape[0] == 1, "single-layer block"
    b, l, dm = x.shape
    lc = ctx.shape[1]
    s5w = s5_d.shape[1]
    hyw = hy_d.shape[1]
    n_exp = w_router.shape[2]
    row = lambda a: a.reshape(1, -1)

    cc = jnp.concatenate([c, c_ctx[None], jnp.zeros((2 * SUBLANES - b - 1, dm), F32)], axis=0)
    mod = _adaln(cc, w_ada[0], row(b_ada[0]))
    sh1, sc1, ga1, sh2, sc2, ga2 = [mod[:b, k * dm:(k + 1) * dm].reshape(b, 1, dm) for k in range(N_MOD)]
    csh1, csc1 = [mod[b:b + 1, k * dm:(k + 1) * dm].reshape(1, 1, dm) for k in range(2)]

    w_in_b = w_in[0].astype(BF16)
    g1 = row(g_norm1[0])
    p = _inproj(x, g1, sh1, sc1, w_in_b, s5w, tm=512)
    p_ctx = _inproj(ctx, g1, csh1, csc1, w_in_b, s5w, tm=lc)
    v, x0 = _inproj_hyena(x, g1, sh1, sc1, w_in_b, hy_conv_w[0], row(hy_conv_b[0]), hyw, s5w)

    bw, cw, lamr, lami = _s5_weights(s5_lam_re[0], s5_lam_im[0], s5_log_step[0],
                                     s5_b_re[0], s5_b_im[0], s5_c_re[0], s5_c_im[0])
    zero_state = jnp.zeros((2, bw.shape[1], SUBLANES, 2 * MODES_PER_BLOCK), F32)
    (ctx_state,) = _s5_scan(p_ctx, bw, cw, lamr, lami, zero_state, tl=64, emit_y=False)
    y_s5, _ = _s5_scan(p, bw, cw, lamr, lami, ctx_state, tl=64, emit_y=True)

    ka, kd = _hyena_filter(_filter_features(l), jnp.pad(hy_w1[0], ((0, LANES - hy_w1.shape[1]), (0, 0))),
                           row(hy_b1[0]), hy_w2[0], row(hy_b2[0]), hy_w3[0], row(hy_b3[0]),
                           row(hy_freq[0]), hy_w4[0], row(hy_deltas[0]))
    ctab, stab = _dft_tables(l)
    kk, kn = _hyena_spectrum(ctab, stab, ka, kd)
    y_hy = _hyena_conv(v, x0, kk, kn, row(hy_d[0]), ctab, stab)

    w_rh, w_rl = _split_bf16(jnp.pad(w_router[0], ((0, 0), (0, LANES - n_exp))))
    b_r = jnp.concatenate([b_router[0], jnp.full((LANES - n_exp,), NEG_BIG, F32)]).reshape(1, LANES)
    x1, h2, te, gate, cnt = _outproj(y_s5, p, y_hy, x, row(s5_d[0]), s5_w_glu[0].astype(BF16),
                                     row(s5_b_glu[0]), w_out[0].astype(BF16), ga1, row(g_norm2[0]),
                                     sh2, sc2, w_rh, w_rl, b_r)

    t = b * l
    n_blocks = t * TOP_K // ROW_BLOCK + n_exp
    counts = cnt[0, :n_exp].astype(I32)
    seg_start, first_blk, n_blk, total, pad_start, pad_len = _segments(counts)
    seg_row = jnp.broadcast_to(jnp.pad(seg_start.astype(F32), (0, LANES - n_exp)), (SUBLANES, LANES))
    pos = _slots(te.reshape(t, LANES), seg_row)
    pos_flat = pos[:, :TOP_K].reshape(-1)
    xs = _dispatch(pos_flat, pad_start, pad_len, h2.reshape(t, dm // LANES, LANES), n_blocks * ROW_BLOCK)
    hid = _expert_gateup(first_blk, n_blk, total, xs, w_gate_up[0], b_gate_up[0].reshape(n_exp, 1, -1))
    y_e = _expert_down(first_blk, n_blk, total, hid, w_down[0], b_down[0].reshape(n_exp, 1, -1))
    return _combine(pos_flat, x1, gate, ga2, row(g_final), y_e)
```

```python
import functools
import math

import jax
import jax.numpy as jnp
from jax import lax
from jax.experimental import pallas as pl
from jax.experimental.pallas import tpu as pltpu

F32 = jnp.float32
BF16 = jnp.bfloat16
I32 = jnp.int32
U32 = jnp.uint32
HIGHEST = lax.Precision.HIGHEST

GRID_W = 64
S5_H = 16
S5_P = 64
S5_RE_MAX = -1e-4
HY_BANDS = 16
N_EXPERTS = 32
TOP_K = 4
SWIGLU_LIMIT = 7.0
SWIGLU_ALPHA = 1.702
N_MOD = 6
EPS = 1e-6

LANES = 128
SUBLANES = 8
GROUPS_PER_BLOCK = LANES // S5_H
MODES_PER_BLOCK = GROUPS_PER_BLOCK * S5_P
ROW_BLOCK = 256
IN_SLOTS = 3
NEG_BIG = -1e30


def _cparams(sem, vmem_mb):
    return pltpu.CompilerParams(dimension_semantics=sem, vmem_limit_bytes=vmem_mb << 20)


def _dot(a, b):
    return jnp.dot(a, b, preferred_element_type=F32)


def _dot_hi(a, b):
    return jnp.dot(a, b, preferred_element_type=F32, precision=HIGHEST)


def _adaln_body(c_ref, w_ref, b_ref, o_ref):
    c = c_ref[...]
    s = c * jax.nn.sigmoid(c)
    o_ref[...] = _dot_hi(s, w_ref[...]) + b_ref[...]


def _adaln(cc, w, b, tn=1024):
    rows, d = cc.shape
    n = w.shape[1]
    return pl.pallas_call(
        _adaln_body,
        grid=(n // tn,),
        in_specs=[pl.BlockSpec((rows, d), lambda j: (0, 0)),
                  pl.BlockSpec((d, tn), lambda j: (0, j)),
                  pl.BlockSpec((1, tn), lambda j: (0, j))],
        out_specs=pl.BlockSpec((rows, tn), lambda j: (0, j)),
        out_shape=jax.ShapeDtypeStruct((rows, n), F32),
        compiler_params=_cparams(("arbitrary",), 40),
        name="adaln",
    )(cc, w, b)


def _norm_mod(x, g, sh, sc):
    ms = jnp.mean(x * x, axis=-1, keepdims=True)
    return (x * lax.rsqrt(ms + EPS) * g) * (1.0 + sc) + sh


def _inproj_body(x_ref, g_ref, sh_ref, sc_ref, w_ref, o_ref):
    half = x_ref.shape[1] // 2
    for r0 in (0, half):
        rs = slice(r0, r0 + half)
        h = _norm_mod(x_ref[0, rs, :], g_ref[...], sh_ref[0], sc_ref[0])
        o_ref[0, rs, :] = _dot(h.astype(BF16), w_ref[...])


def _inproj(x, g, sh, sc, w, n_cols, tm, tn=1024):
    b, l, d = x.shape
    if sh.shape[0] > 1:
        bidx = lambda n, bi, li: (bi, 0, 0)
    else:
        bidx = lambda n, bi, li: (0, 0, 0)
    return pl.pallas_call(
        _inproj_body,
        grid=(n_cols // tn, b, l // tm),
        in_specs=[pl.BlockSpec((1, tm, d), lambda n, bi, li: (bi, li, 0)),
                  pl.BlockSpec((1, d), lambda n, bi, li: (0, 0)),
                  pl.BlockSpec((1, 1, d), bidx),
                  pl.BlockSpec((1, 1, d), bidx),
                  pl.BlockSpec((d, tn), lambda n, bi, li: (0, n))],
        out_specs=pl.BlockSpec((1, tm, tn), lambda n, bi, li: (bi, li, n)),
        out_shape=jax.ShapeDtypeStruct((b, l, n_cols), F32),
        compiler_params=_cparams(("arbitrary", "arbitrary", "arbitrary"), 40),
        name="inproj",
    )(x, g, sh, sc, w)


def _s5prep_body(lre_ref, lim_ref, ls_ref, bre_ref, bim_ref, lbr_ref, lbi_ref, bbr_ref, bbi_ref):
    lr = jnp.minimum(lre_ref[...], S5_RE_MAX)
    li = lim_ref[...]
    dt = jnp.exp(ls_ref[...])
    er = jnp.exp(lr * dt)
    lbr = er * jnp.cos(li * dt)
    lbi = er * jnp.sin(li * dt)
    nr = lbr - 1.0
    den = lr * lr + li * li
    qr = (nr * lr + lbi * li) / den
    qi = (lbi * lr - nr * li) / den
    bre = bre_ref[...]
    bim = bim_ref[...]
    lbr_ref[...] = lbr
    lbi_ref[...] = lbi
    bbr_ref[...] = qr * bre - qi * bim
    bbi_ref[...] = qr * bim + qi * bre


def _s5_weights(lam_re, lam_im, log_step, b_re, b_im, c_re, c_im):
    nd, g, p = lam_re.shape
    h = S5_H
    nj = g // GROUPS_PER_BLOCK
    rep = lambda a: jnp.repeat(a.reshape(nd * g, 1, p), h, axis=1).reshape(nd * g * h, p)
    tr = lambda a: jnp.transpose(a, (0, 1, 3, 2)).reshape(nd * g * h, p)
    shp = jax.ShapeDtypeStruct((nd * g * h, p), F32)
    lbr, lbi, bbr, bbi = pl.pallas_call(
        _s5prep_body, out_shape=(shp, shp, shp, shp), name="s5prep",
    )(rep(lam_re), rep(lam_im), rep(log_step), tr(b_re), tr(b_im))
    eye = jnp.eye(GROUPS_PER_BLOCK, dtype=F32)
    bb = jnp.stack([bbr, bbi]).reshape(2, nd, nj, GROUPS_PER_BLOCK, h, p)
    bw = jnp.einsum('rdjahp,ab->djahrbp', bb, eye).reshape(nd, nj, LANES, 2 * MODES_PER_BLOCK)
    cc = jnp.stack([c_re, -c_im]).reshape(2, nd, nj, GROUPS_PER_BLOCK, h, p)
    cw = jnp.einsum('rdjahp,ab->djrapbh', cc, eye).reshape(nd, nj, 2 * MODES_PER_BLOCK, LANES)
    pick = lambda a: a.reshape(nd, g, h, p)[:, :, 0, :].reshape(nd, nj, 1, MODES_PER_BLOCK)
    return bw.astype(BF16), cw.astype(BF16), pick(lbr), pick(lbi)


def _s5_body(*refs, tl, nt, nj, emit_y):
    if emit_y:
        (u_hbm, bw_ref, cw_ref, lr_ref, li_ref, init_ref, y_hbm, fin_ref,
         u_scr, bu_scr, st_scr, in_sem, y_scr, out_sem) = refs
    else:
        (u_hbm, bw_ref, lr_ref, li_ref, init_ref, fin_ref, u_scr, bu_scr, st_scr, in_sem) = refs
    g = pl.program_id(0)
    ng = pl.num_programs(0)
    d = g // nt
    t = g % nt
    slot = g % 2
    nb = SUBLANES
    mpb = MODES_PER_BLOCK
    c = nj * LANES
    rows = tl * nb

    def tile_of(step):
        dd = step // nt
        tt = step % nt
        return dd, tt + dd * (nt - 1 - 2 * tt)

    def in_copies(step, sl):
        _, tile = tile_of(step)
        return [pltpu.make_async_copy(u_hbm.at[b, pl.ds(tile * tl, tl), pl.ds(0, c)],
                                      u_scr.at[sl, :, b, :], in_sem.at[sl]) for b in range(nb)]

    def out_copies(step, sl):
        dd, tile = tile_of(step)
        return [pltpu.make_async_copy(y_scr.at[sl, :, b, :], y_hbm.at[dd, b, pl.ds(tile * tl, tl), :],
                                      out_sem.at[sl]) for b in range(nb)]

    @pl.when(g == 0)
    def _():
        for cp in in_copies(0, 0):
            cp.start()

    @pl.when(g + 1 < ng)
    def _():
        for cp in in_copies(g + 1, (g + 1) % 2):
            cp.start()

    for cp in in_copies(g, slot):
        cp.wait()

    if emit_y:
        @pl.when(g >= 2)
        def _():
            for cp in out_copies(g - 2, slot):
                cp.wait()

    @pl.when(t == 0)
    def _():
        st_scr[...] = init_ref[0]

    def project(j):
        s = u_scr[slot, :, :, j * LANES:(j + 1) * LANES].reshape(rows, LANES)
        bu_scr[j % 3] = _dot(s.astype(BF16), bw_ref[0, j])

    def readout(j):
        y = _dot(bu_scr[j % 3].astype(BF16), cw_ref[0, j])
        y_scr[slot, :, :, j * LANES:(j + 1) * LANES] = y.reshape(tl, nb, LANES)

    project(0)
    for j in range(nj):
        if j + 1 < nj:
            project(j + 1)
        if emit_y and j >= 1:
            readout(j - 1)
        lr = jnp.broadcast_to(lr_ref[0, j], (nb, mpb))
        li = jnp.broadcast_to(li_ref[0, j], (nb, mpb))
        cur = j % 3
        xr = st_scr[j, :, 0:mpb]
        xi = st_scr[j, :, mpb:2 * mpb]
        for s in range(tl):
            r0 = pl.multiple_of((s + d * (tl - 1 - 2 * s)) * nb, nb)
            br = bu_scr[cur, pl.ds(r0, nb), 0:mpb]
            bi = bu_scr[cur, pl.ds(r0, nb), mpb:2 * mpb]
            xr, xi = lr * xr - li * xi + br, lr * xi + li * xr + bi
            bu_scr[cur, pl.ds(r0, nb), 0:mpb] = xr
            bu_scr[cur, pl.ds(r0, nb), mpb:2 * mpb] = xi
        st_scr[j, :, 0:mpb] = xr
        st_scr[j, :, mpb:2 * mpb] = xi

    if emit_y:
        readout(nj - 1)
        for cp in out_copies(g, slot):
            cp.start()

        @pl.when(g == ng - 1)
        def _():
            for cp in out_copies(g - 1, 1 - slot):
                cp.wait()
            for cp in out_copies(g, slot):
                cp.wait()

    @pl.when(t == nt - 1)
    def _():
        fin_ref[0] = st_scr[...]


def _s5_scan(u_arr, bw, cw, lamr, lami, init, tl, emit_y):
    b, l = u_arr.shape[:2]
    assert b == SUBLANES and l % tl == 0
    nj = bw.shape[1]
    c = nj * LANES
    nt = l // tl
    m2 = 2 * MODES_PER_BLOCK
    wmap = lambda g: (g // nt, 0, 0, 0)
    in_specs = [pl.BlockSpec(memory_space=pl.ANY), pl.BlockSpec((1, nj, LANES, m2), wmap)]
    args = [u_arr, bw]
    if emit_y:
        in_specs.append(pl.BlockSpec((1, nj, m2, LANES), wmap))
        args.append(cw)
    in_specs += [pl.BlockSpec((1, nj, 1, MODES_PER_BLOCK), wmap),
                 pl.BlockSpec((1, nj, 1, MODES_PER_BLOCK), wmap),
                 pl.BlockSpec((1, nj, SUBLANES, m2), wmap)]
    args += [lamr, lami, init]
    fin_spec = pl.BlockSpec((1, nj, SUBLANES, m2), wmap)
    fin_shape = jax.ShapeDtypeStruct((2, nj, SUBLANES, m2), F32)
    scratch = [pltpu.VMEM((2, tl, SUBLANES, c), F32),
               pltpu.VMEM((3, tl * SUBLANES, m2), F32),
               pltpu.VMEM((nj, SUBLANES, m2), F32),
               pltpu.SemaphoreType.DMA((2,))]
    if emit_y:
        out_specs = [pl.BlockSpec(memory_space=pl.ANY), fin_spec]
        out_shape = [jax.ShapeDtypeStruct((2, b, l, c), F32), fin_shape]
        scratch += [pltpu.VMEM((2, tl, SUBLANES, c), F32), pltpu.SemaphoreType.DMA((2,))]
    else:
        out_specs = [fin_spec]
        out_shape = [fin_shape]
    return pl.pallas_call(
        functools.partial(_s5_body, tl=tl, nt=nt, nj=nj, emit_y=emit_y),
        grid=(2 * nt,),
        in_specs=in_specs, out_specs=out_specs, out_shape=out_shape,
        scratch_shapes=scratch,
        compiler_params=_cparams(("arbitrary",), 48),
        name="s5scan_lat" if emit_y else "s5scan_ctx",
    )(*args)


def _hyfilt_body(z_ref, w1_ref, b1_ref, w2_ref, b2_ref, w3_ref, b3_ref, fr_ref,
                 w4f_ref, w4b_ref, df_ref, db_ref, ka_ref, kd_ref, h_scr, *, length):
    z = z_ref[...]

    @pl.when(pl.program_id(0) == 0)
    def _():
        fr = fr_ref[...]
        h = jnp.sin(fr * (_dot_hi(z, w1_ref[...]) + b1_ref[...]))
        h = jnp.sin(fr * (_dot_hi(h, w2_ref[...]) + b2_ref[...]))
        h_scr[...] = jnp.sin(fr * (_dot_hi(h, w3_ref[...]) + b3_ref[...]))

    h = h_scr[...]
    t = z[:, 0:1]
    hf = _dot_hi(h, w4f_ref[...]) * jnp.exp(-t * jnp.abs(df_ref[...]))
    hb = _dot_hi(h, w4b_ref[...]) * jnp.exp(-t * jnp.abs(db_ref[...]))
    row = lax.broadcasted_iota(I32, hf.shape, 0)
    hb = jnp.where(row < length - 1, hb, 0.0)
    norm = jnp.sum(jnp.abs(hf), axis=0, keepdims=True) + jnp.sum(jnp.abs(hb), axis=0, keepdims=True)
    kf = hf / norm
    kb = jnp.where(row >= 1, pltpu.roll(hb, 1, axis=0), 0.0) / norm
    ka_ref[...] = kf + kb
    kd_ref[...] = kf - kb


def _hyena_filter(z, w1, b1, w2, b2, w3, b3, fr, w4, deltas, ct=256):
    length, ze = z.shape
    order = w2.shape[0]
    c = w4.shape[1] // 2
    nct = c // ct
    full = lambda shape: pl.BlockSpec(shape, lambda i: (0, 0))
    shp = jax.ShapeDtypeStruct((length, c), F32)
    return pl.pallas_call(
        functools.partial(_hyfilt_body, length=length),
        grid=(nct,),
        in_specs=[full((length, ze)), full((ze, order)), full((1, order)),
                  full((order, order)), full((1, order)), full((order, order)), full((1, order)),
                  full((1, order)),
                  pl.BlockSpec((order, ct), lambda i: (0, i)),
                  pl.BlockSpec((order, ct), lambda i: (0, nct + i)),
                  pl.BlockSpec((1, ct), lambda i: (0, i)),
                  pl.BlockSpec((1, ct), lambda i: (0, nct + i))],
        out_specs=[pl.BlockSpec((length, ct), lambda i: (0, i))] * 2,
        out_shape=[shp, shp],
        scratch_shapes=[pltpu.VMEM((length, order), F32)],
        compiler_params=_cparams(("arbitrary",), 48),
        name="hyena_filter",
    )(z, w1, b1, w2, b2, w3, b3, fr, w4, w4, deltas, deltas)


def _dfttab_body(ca_ref, sa_ref, cb_ref, sb_ref, c_ref, s_ref):
    ca = ca_ref[0]
    sa = sa_ref[0]
    cb = cb_ref[...]
    sb = sb_ref[...]
    c_ref[...] = (ca * cb - sa * sb).astype(c_ref.dtype)
    s_ref[...] = (sa * cb + ca * sb).astype(s_ref.dtype)


def _dft_tables(length, r=128):
    n = 2 * length
    s = jnp.arange(length, dtype=I32)[None, :]
    ang = lambda k: (2.0 * math.pi / n) * ((k * s) % n).astype(F32)
    aa = ang(jnp.arange(length // r, dtype=I32)[:, None] * r)
    ab = ang(jnp.arange(r, dtype=I32)[:, None])
    rows3 = lambda a: a.reshape(length // r, 1, length)
    shp = jax.ShapeDtypeStruct((length, length), BF16)
    return pl.pallas_call(
        _dfttab_body,
        grid=(length // r,),
        in_specs=[pl.BlockSpec((None, 1, length), lambda i: (i, 0, 0)),
                  pl.BlockSpec((None, 1, length), lambda i: (i, 0, 0)),
                  pl.BlockSpec((r, length), lambda i: (0, 0)),
                  pl.BlockSpec((r, length), lambda i: (0, 0))],
        out_specs=[pl.BlockSpec((r, length), lambda i: (i, 0))] * 2,
        out_shape=[shp, shp],
        compiler_params=_cparams(("arbitrary",), 32),
        name="dft_tables",
    )(rows3(jnp.cos(aa)), rows3(jnp.sin(aa)), jnp.cos(ab), jnp.sin(ab))


def _hyspec_body(c_ref, s_ref, ka_ref, kd_ref, kk_ref, kn_ref, *, length, fq):
    q = pl.program_id(1)
    nq = length // fq
    n = 2.0 * length

    @pl.when(q < nq)
    def _():
        grow = q * fq + lax.broadcasted_iota(I32, (fq, 1), 0)
        scale = jnp.where(grow == 0, 1.0 / n, 2.0 / n)
        kk_ref[...] = _dot(c_ref[...], ka_ref[...].astype(BF16)) * scale

    @pl.when(q >= nq)
    def _():
        kk_ref[...] = _dot(s_ref[...], kd_ref[...].astype(BF16)) * (2.0 / n)

    @pl.when(q == 0)
    def _():
        a = ka_ref[...]
        row = lax.broadcasted_iota(I32, a.shape, 0)
        sgn = jnp.where(row % 2 == 0, 1.0, -1.0)
        kn = jnp.sum(a * sgn, axis=0, keepdims=True) * (1.0 / n)
        kn_ref[...] = jnp.broadcast_to(kn, kn_ref.shape)


def _hyena_spectrum(ctab, stab, ka, kd, ct=256, fq=512):
    length, c = ka.shape
    nq = length // fq
    return pl.pallas_call(
        functools.partial(_hyspec_body, length=length, fq=fq),
        grid=(c // ct, 2 * nq),
        in_specs=[pl.BlockSpec((fq, length), lambda i, q: (jnp.minimum(q, nq - 1), 0)),
                  pl.BlockSpec((fq, length), lambda i, q: (jnp.maximum(q - nq, 0), 0)),
                  pl.BlockSpec((length, ct), lambda i, q: (0, i)),
                  pl.BlockSpec((length, ct), lambda i, q: (0, i))],
        out_specs=[pl.BlockSpec((fq, ct), lambda i, q: (q, i)),
                   pl.BlockSpec((SUBLANES, ct), lambda i, q: (0, i))],
        out_shape=[jax.ShapeDtypeStruct((2 * length, c), F32), jax.ShapeDtypeStruct((SUBLANES, c), F32)],
        compiler_params=_cparams(("arbitrary", "arbitrary"), 48),
        name="hyena_spectrum",
    )(ctab, stab, ka, kd)


def _inproj_hy_body(x_ref, g_ref, sh_ref, sc_ref, wv_ref, w1_ref, w0_ref, cv_ref, c1_ref, c0_ref,
                    bv_ref, b1_ref, b0_ref, v_ref, x0_ref):
    half = x_ref.shape[1] // 2
    row = lax.broadcasted_iota(I32, (half, 1), 0) % GRID_W

    def sconv(a, w_ref, b_ref):
        prev = jnp.where(row == 0, 0.0, pltpu.roll(a, 1, axis=0))
        nxt = jnp.where(row == GRID_W - 1, 0.0, pltpu.roll(a, half - 1, axis=0))
        return prev * w_ref[0:1, :] + a * w_ref[1:2, :] + nxt * w_ref[2:3, :] + b_ref[...]

    for r0 in (0, half):
        rs = slice(r0, r0 + half)
        h = _norm_mod(x_ref[0, rs, :], g_ref[...], sh_ref[0], sc_ref[0]).astype(BF16)
        zv = sconv(_dot(h, wv_ref[...]), cv_ref, bv_ref)
        z1 = sconv(_dot(h, w1_ref[...]), c1_ref, b1_ref)
        v_ref[0, rs, :] = (zv * z1).astype(v_ref.dtype)
        x0_ref[0, rs, :] = sconv(_dot(h, w0_ref[...]), c0_ref, b0_ref).astype(x0_ref.dtype)


def _inproj_hyena(x, g, sh, sc, w, conv_w, conv_b, c, first_col, tm=512, ct=256):
    b, l, d = x.shape
    assert tm % (2 * GRID_W) == 0
    nct = c // ct
    base = first_col // ct
    tokmap = lambda ci, bi, li: (bi, li, 0)
    perb = pl.BlockSpec((1, 1, d), lambda ci, bi, li: (bi, 0, 0))
    wspec = lambda k: pl.BlockSpec((d, ct), lambda ci, bi, li: (0, base + k * nct + ci))
    cspec = lambda k: pl.BlockSpec((3, ct), lambda ci, bi, li: (0, k * nct + ci))
    bspec = lambda k: pl.BlockSpec((1, ct), lambda ci, bi, li: (0, k * nct + ci))
    shp = jax.ShapeDtypeStruct((b, l, c), BF16)
    return pl.pallas_call(
        _inproj_hy_body,
        grid=(nct, b, l // tm),
        in_specs=[pl.BlockSpec((1, tm, d), tokmap), pl.BlockSpec((1, d), lambda ci, bi, li: (0, 0)), perb, perb,
                  wspec(0), wspec(1), wspec(2), cspec(0), cspec(1), cspec(2), bspec(0), bspec(1), bspec(2)],
        out_specs=[pl.BlockSpec((1, tm, ct), lambda ci, bi, li: (bi, li, ci))] * 2,
        out_shape=[shp, shp],
        compiler_params=_cparams(("arbitrary", "arbitrary", "arbitrary"), 40),
        name="inproj_hyena",
    )(x, g, sh, sc, w, w, w, conv_w, conv_w, conv_w, conv_b, conv_b, conv_b)


def _hyconv_body(v_ref, x0_ref, kk_ref, kn_ref, d_ref, c_hbm, s_hbm, o_ref, c_scr, s_scr, sem, *, length, fq):
    first = jnp.logical_and(pl.program_id(0) == 0, pl.program_id(1) == 0)

    @pl.when(first)
    def _():
        cc = pltpu.make_async_copy(c_hbm, c_scr, sem.at[0])
        cs = pltpu.make_async_copy(s_hbm, s_scr, sem.at[1])
        cc.start()
        cs.start()
        cc.wait()
        cs.wait()

    vb = v_ref[0]
    v = vb.astype(F32)
    row = lax.broadcasted_iota(I32, (length, 1), 0)
    sgn = jnp.where(row % 2 == 0, 1.0, -1.0)
    vn = jnp.sum(v * sgn, axis=0, keepdims=True)
    acc = sgn * (vn * kn_ref[0:1, :])
    for q in range(length // fq):
        lo, hi = q * fq, (q + 1) * fq
        vr = _dot(c_scr[lo:hi, :], vb)
        vs = _dot(s_scr[lo:hi, :], vb)
        kr = kk_ref[lo:hi, :]
        ks = kk_ref[length + lo:length + hi, :]
        pr = vr * kr - vs * ks
        ps = vr * ks + vs * kr
        acc = acc + _dot(c_scr[:, lo:hi], pr.astype(BF16)) + _dot(s_scr[:, lo:hi], ps.astype(BF16))
    o_ref[0] = ((acc + v * d_ref[...]) * x0_ref[0].astype(F32)).astype(o_ref.dtype)


def _hyena_conv(v, x0, kk, kn, d_skip, ctab, stab, ct=256, fq=256):
    b, l, c = v.shape
    return pl.pallas_call(
        functools.partial(_hyconv_body, length=l, fq=fq),
        grid=(c // ct, b),
        in_specs=[pl.BlockSpec((1, l, ct), lambda ci, bi: (bi, 0, ci)),
                  pl.BlockSpec((1, l, ct), lambda ci, bi: (bi, 0, ci)),
                  pl.BlockSpec((2 * l, ct), lambda ci, bi: (0, ci)),
                  pl.BlockSpec((SUBLANES, ct), lambda ci, bi: (0, ci)),
                  pl.BlockSpec((1, ct), lambda ci, bi: (0, ci)),
                  pl.BlockSpec(memory_space=pl.ANY),
                  pl.BlockSpec(memory_space=pl.ANY)],
        out_specs=pl.BlockSpec((1, l, ct), lambda ci, bi: (bi, 0, ci)),
        out_shape=jax.ShapeDtypeStruct((b, l, c), BF16),
        scratch_shapes=[pltpu.VMEM((l, l), BF16), pltpu.VMEM((l, l), BF16),
                        pltpu.SemaphoreType.DMA((2,))],
        compiler_params=_cparams(("arbitrary", "arbitrary"), 48),
        name="hyena_conv",
    )(v, x0, kk, kn, d_skip, ctab, stab)


def _split_bf16(a):
    hi = a.astype(BF16)
    return hi, (a - hi.astype(F32)).astype(BF16)


def _outproj_body(y_ref, u_ref, yh_ref, x_ref, d_ref, wg_ref, bg_ref, wo_ref, ga_ref, g2_ref,
                  sh_ref, sc_ref, wrh_ref, wrl_ref, br_ref, x1_ref, hp_ref, te_ref, gt_ref, cnt_ref, *, sub):
    tm, c = u_ref.shape[1], u_ref.shape[2]
    dm = x_ref.shape[2]

    @pl.when(jnp.logical_and(pl.program_id(0) == 0, pl.program_id(1) == 0))
    def _():
        cnt_ref[...] = jnp.zeros_like(cnt_ref)

    lane = lax.broadcasted_iota(I32, (sub, LANES), 1)
    total = jnp.zeros((1, LANES), F32)
    for r0 in range(0, tm, sub):
        rs = slice(r0, r0 + sub)
        ys = y_ref[0, 0, rs, :] + y_ref[1, 0, rs, :] + u_ref[0, rs, :] * d_ref[...]
        ge = 0.5 * ys * (1.0 + lax.erf(ys * (1.0 / math.sqrt(2.0))))
        gl = ge * jax.nn.sigmoid(_dot(ge.astype(BF16), wg_ref[...]) + bg_ref[...])
        mix = _dot(gl.astype(BF16), wo_ref[0:c, :]) + _dot(yh_ref[0, rs, :], wo_ref[c:2 * c, :])
        x1 = x_ref[0, rs, :] + ga_ref[0] * mix
        x1_ref[0, rs, :] = x1
        h2 = _norm_mod(x1, g2_ref[...], sh_ref[0], sc_ref[0])
        hp_ref[0, rs] = h2.astype(BF16).reshape(sub, dm // LANES, LANES)

        hh, hl = _split_bf16(h2)
        logits = _dot(hh, wrh_ref[...]) + (_dot(hh, wrl_ref[...]) + _dot(hl, wrh_ref[...])) + br_ref[...]
        work = logits
        vals, idxs = [], []
        for _ in range(TOP_K):
            m = jnp.max(work, axis=-1, keepdims=True)
            ix = jnp.min(jnp.where(work == m, lane, LANES), axis=-1, keepdims=True)
            vals.append(m)
            idxs.append(ix)
            work = jnp.where(lane == ix, NEG_BIG, work)
        ex = [jnp.exp(v - vals[0]) for v in vals]
        den = ex[0]
        for e in ex[1:]:
            den = den + e
        te = jnp.zeros((sub, LANES), I32)
        gt = jnp.zeros((sub, LANES), F32)
        mh = jnp.zeros((sub, LANES), F32)
        for k in range(TOP_K):
            te = jnp.where(lane == k, idxs[k], te)
            gt = jnp.where(lane == k, ex[k] / den, gt)
            mh = mh + (lane == idxs[k]).astype(F32)
        te_ref[0, rs, :] = te
        gt_ref[0, rs, :] = gt
        total = total + jnp.sum(mh, axis=0, keepdims=True)

    cnt_ref[...] += jnp.broadcast_to(total, cnt_ref.shape)


def _outproj(y, p, yh, x, d_skip, w_glu, b_glu, w_out, ga1, g2, sh2, sc2, w_rh, w_rl, b_r, tm=256, sub=128):
    b, l, dm = x.shape
    c = yh.shape[2]
    tok = lambda width: pl.BlockSpec((1, tm, width), lambda bi, li: (bi, li, 0))
    full = lambda shape: pl.BlockSpec(shape, lambda bi, li: (0,) * len(shape))
    perb = pl.BlockSpec((1, 1, dm), lambda bi, li: (bi, 0, 0))
    return pl.pallas_call(
        functools.partial(_outproj_body, sub=sub),
        grid=(b, l // tm),
        in_specs=[pl.BlockSpec((2, 1, tm, c), lambda bi, li: (0, bi, li, 0)),
                  tok(c), tok(c), tok(dm),
                  full((1, c)), full((c, c)), full((1, c)), full((2 * c, dm)),
                  perb, full((1, dm)), perb, perb,
                  full((dm, LANES)), full((dm, LANES)), full((1, LANES))],
        out_specs=[tok(dm), pl.BlockSpec((1, tm, dm // LANES, LANES), lambda bi, li: (bi, li, 0, 0)),
                   tok(LANES), tok(LANES), full((SUBLANES, LANES))],
        out_shape=[jax.ShapeDtypeStruct((b, l, dm), F32),
                   jax.ShapeDtypeStruct((b, l, dm // LANES, LANES), BF16),
                   jax.ShapeDtypeStruct((b, l, LANES), I32),
                   jax.ShapeDtypeStruct((b, l, LANES), F32),
                   jax.ShapeDtypeStruct((SUBLANES, LANES), F32)],
        compiler_params=_cparams(("arbitrary", "arbitrary"), 56),
        name="outproj_router",
    )(y, p, yh, x, d_skip, w_glu, b_glu, w_out, ga1, g2, sh2, sc2, w_rh, w_rl, b_r)


def _slots_body(te_ref, seg_ref, pos_ref, carry):
    @pl.when(pl.program_id(0) == 0)
    def _():
        carry[...] = jnp.zeros_like(carry)

    te = te_ref[...]
    tm = te.shape[0]
    lane = lax.broadcasted_iota(I32, (tm, LANES), 1)
    sel = [lane == te[:, k:k + 1] for k in range(TOP_K)]
    mh = jnp.zeros((tm, LANES), F32)
    for s in sel:
        mh = mh + s.astype(F32)
    r = lax.broadcasted_iota(I32, (tm, tm), 0)
    cc = lax.broadcasted_iota(I32, (tm, tm), 1)
    strict_lower = (r > cc).astype(BF16)
    slot = _dot(strict_lower, mh.astype(BF16)) + carry[0:1, :] + seg_ref[0:1, :]
    pos = jnp.zeros((tm, LANES), F32)
    for k in range(TOP_K):
        pk = jnp.sum(jnp.where(sel[k], slot, 0.0), axis=-1, keepdims=True)
        pos = jnp.where(lane == k, pk, pos)
    pos_ref[...] = pos.astype(I32)
    carry[...] += jnp.broadcast_to(jnp.sum(mh, axis=0, keepdims=True), carry.shape)


def _slots(te, seg_start, tm=512):
    t = te.shape[0]
    return pl.pallas_call(
        _slots_body,
        grid=(t // tm,),
        in_specs=[pl.BlockSpec((tm, LANES), lambda i: (i, 0)),
                  pl.BlockSpec((SUBLANES, LANES), lambda i: (0, 0))],
        out_specs=pl.BlockSpec((tm, LANES), lambda i: (i, 0)),
        out_shape=jax.ShapeDtypeStruct((t, LANES), I32),
        scratch_shapes=[pltpu.VMEM((SUBLANES, LANES), F32)],
        compiler_params=_cparams(("arbitrary",), 32),
        name="moe_slots",
    )(te, seg_start)


PAD_CHUNKS = tuple(ROW_BLOCK >> k for k in range(1, ROW_BLOCK.bit_length()))


def _row_copy(src, src_row, dst, dst_row, sem):
    return pltpu.make_async_copy(src.at[pl.ds(src_row, 1)], dst.at[pl.ds(dst_row, 1)], sem)


def _dispatch_body(pos_ref, ps_ref, pn_ref, h_ref, xs_out, zbuf, sem, zsem, *, tm, n_exp, n_blocks):
    @pl.when(pl.program_id(0) == 0)
    def _():
        zbuf[...] = jnp.zeros_like(zbuf)

        def copies(e):
            start = ps_ref[e]
            length = pn_ref[e]
            out = []
            for size in PAD_CHUNKS:
                done = length & (-2 * size)
                dst = xs_out.at[pl.ds(start + done, size)]
                out.append(((length & size) != 0, pltpu.make_async_copy(zbuf.at[pl.ds(0, size)], dst, zsem)))
            return out

        def fill(e, carry):
            for on, cp in copies(e):
                pl.when(on)(cp.start)
            return carry

        def drain(e, carry):
            for on, cp in copies(e):
                pl.when(on)(cp.wait)
            return carry

        lax.fori_loop(0, n_exp, fill, 0)
        lax.fori_loop(0, n_exp, drain, 0)

        def tail(blk):
            rows = [blk * ROW_BLOCK + part * PAD_CHUNKS[0] for part in range(ROW_BLOCK // PAD_CHUNKS[0])]
            return [pltpu.make_async_copy(zbuf, xs_out.at[pl.ds(r, PAD_CHUNKS[0])], zsem) for r in rows]

        def fill_tail(blk, carry):
            for cp in tail(blk):
                cp.start()
            return carry

        def drain_tail(blk, carry):
            for cp in tail(blk):
                cp.wait()
            return carry

        first_unused = (ps_ref[n_exp - 1] + pn_ref[n_exp - 1]) // ROW_BLOCK
        lax.fori_loop(first_unused, n_blocks, fill_tail, 0)
        lax.fori_loop(first_unused, n_blocks, drain_tail, 0)

    base = pl.program_id(0) * (tm * TOP_K)

    def issue(r, carry):
        for k in range(TOP_K):
            _row_copy(h_ref, r, xs_out, pos_ref[base + r * TOP_K + k], sem).start(priority=k % 2)
        return carry

    lax.fori_loop(0, tm, issue, 0, unroll=8)

    def drain_rows(r, carry):
        for k in range(TOP_K):
            _row_copy(h_ref, 0, xs_out, 0, sem).wait()
        return carry

    lax.fori_loop(0, tm, drain_rows, 0, unroll=8)


def _dispatch(pos_flat, pad_start, pad_len, hp, n_rows, tm=256):
    t = hp.shape[0]
    tile = hp.shape[1:]
    return pl.pallas_call(
        functools.partial(_dispatch_body, tm=tm, n_exp=pad_start.shape[0], n_blocks=n_rows // ROW_BLOCK),
        grid_spec=pltpu.PrefetchScalarGridSpec(
            num_scalar_prefetch=3,
            grid=(t // tm,),
            in_specs=[pl.BlockSpec((tm,) + tile, lambda i, pos, ps, pn: (i, 0, 0))],
            out_specs=pl.BlockSpec(memory_space=pl.ANY),
            scratch_shapes=[pltpu.VMEM((PAD_CHUNKS[0],) + tile, hp.dtype),
                            pltpu.SemaphoreType.DMA(()), pltpu.SemaphoreType.DMA(())]),
        out_shape=jax.ShapeDtypeStruct((n_rows,) + tile, hp.dtype),
        compiler_params=_cparams(("arbitrary",), 32),
        name="moe_dispatch",
    )(pos_flat, pad_start, pad_len, hp)


def _stream_expert_blocks(first, count, total, n_blocks, src_hbm, dst_hbm, col0, ibuf, obuf, isem, osem, compute):
    e = pl.program_id(1)
    rb = ROW_BLOCK
    n_in = ibuf.shape[0]

    def rows(gs):
        return pl.ds(pl.multiple_of(gs * rb, rb), rb)

    def in_copy(gs):
        slot = gs % n_in
        return pltpu.make_async_copy(src_hbm.at[rows(gs)], ibuf.at[slot], isem.at[slot])

    def out_copy(gs, slot):
        dst = dst_hbm.at[rows(gs)] if col0 is None else dst_hbm.at[rows(gs), pl.ds(col0, obuf.shape[2])]
        return pltpu.make_async_copy(obuf.at[slot], dst, osem.at[slot])

    @pl.when(e == 0)
    def _():
        for gs in range(n_in - 1):
            pl.when(gs < total)(in_copy(gs).start)

    def block(s, carry):
        gs = first + s
        slot = gs % 2

        @pl.when(gs + n_in - 1 < total)
        def _():
            in_copy(gs + n_in - 1).start()

        in_copy(gs).wait()

        @pl.when(gs >= 2)
        def _():
            out_copy(gs - 2, slot).wait()

        compute(ibuf.at[gs % n_in], obuf.at[slot])
        out_copy(gs, slot).start()
        return carry

    lax.fori_loop(0, count, block, 0)

    @pl.when(e == pl.num_programs(1) - 1)
    def _():
        @pl.when(total >= 2)
        def _():
            out_copy(total - 2, total % 2).wait()

        out_copy(total - 1, (total - 1) % 2).wait()
        obuf[0] = jnp.zeros(obuf.shape[1:], obuf.dtype)

        def fill(gs, carry):
            out_copy(gs, 0).start()
            return carry

        def drain(gs, carry):
            out_copy(gs, 0).wait()
            return carry

        lax.fori_loop(total, n_blocks, fill, 0)
        lax.fori_loop(total, n_blocks, drain, 0)


def _gateup_body(fs_ref, ns_ref, tot_ref, xs_hbm, wg_ref, wu_ref, bg_ref, bu_ref, h_hbm,
                 wg_s, wu_s, xbuf, obuf, xsem, osem, *, n_blocks):
    e = pl.program_id(1)
    tn = obuf.shape[2]
    half = tn // 2

    @pl.when(ns_ref[e] > 0)
    def _():
        wg_s[...] = wg_ref[...].astype(BF16)
        wu_s[...] = wu_ref[...].astype(BF16)

    def compute(x_ref, o_ref):
        x = x_ref[...].reshape(ROW_BLOCK, wg_s.shape[0])
        for c0 in (0, half):
            cs = slice(c0, c0 + half)
            g = _dot(x, wg_s[:, cs]) + bg_ref[:, cs]
            u = _dot(x, wu_s[:, cs]) + bu_ref[:, cs]
            g = jnp.minimum(g, SWIGLU_LIMIT)
            u = jnp.clip(u, -SWIGLU_LIMIT, SWIGLU_LIMIT)
            o_ref[:, cs] = ((u + 1.0) * (g * jax.nn.sigmoid(SWIGLU_ALPHA * g))).astype(o_ref.dtype)

    col0 = pl.multiple_of(pl.program_id(0) * tn, tn)
    _stream_expert_blocks(fs_ref[e], ns_ref[e], tot_ref[0], n_blocks, xs_hbm, h_hbm, col0,
                          xbuf, obuf, xsem, osem, compute)


def _expert_gateup(first_blk, n_blk, total, xs, w_gate_up, b_gate_up, tn=1024):
    n_exp, dm, ff2 = w_gate_up.shape
    ff = ff2 // 2
    nr = xs.shape[0]
    nn = ff // tn
    return pl.pallas_call(
        functools.partial(_gateup_body, n_blocks=nr // ROW_BLOCK),
        grid_spec=pltpu.PrefetchScalarGridSpec(
            num_scalar_prefetch=3,
            grid=(nn, n_exp),
            in_specs=[pl.BlockSpec(memory_space=pl.ANY),
                      pl.BlockSpec((None, dm, tn), lambda j, e, fs, ns, tot: (e, 0, j)),
                      pl.BlockSpec((None, dm, tn), lambda j, e, fs, ns, tot: (e, 0, nn + j)),
                      pl.BlockSpec((None, 1, tn), lambda j, e, fs, ns, tot: (e, 0, j)),
                      pl.BlockSpec((None, 1, tn), lambda j, e, fs, ns, tot: (e, 0, nn + j))],
            out_specs=pl.BlockSpec(memory_space=pl.ANY),
            scratch_shapes=[pltpu.VMEM((dm, tn), BF16), pltpu.VMEM((dm, tn), BF16),
                            pltpu.VMEM((IN_SLOTS, ROW_BLOCK) + xs.shape[1:], xs.dtype),
                            pltpu.VMEM((2, ROW_BLOCK, tn), BF16),
                            pltpu.SemaphoreType.DMA((IN_SLOTS,)), pltpu.SemaphoreType.DMA((2,))]),
        out_shape=jax.ShapeDtypeStruct((nr, ff), BF16),
        compiler_params=_cparams(("arbitrary", "arbitrary"), 56),
        name="moe_gateup",
    )(first_blk, n_blk, total, xs, w_gate_up, w_gate_up, b_gate_up, b_gate_up)


def _down_body(fs_ref, ns_ref, tot_ref, h_hbm, w_ref, b_ref, y_hbm, w_s, hbuf, obuf, hsem, osem, *, n_blocks):
    e = pl.program_id(1)

    @pl.when(ns_ref[e] > 0)
    def _():
        w_s[...] = w_ref[...].astype(BF16)

    def compute(h_ref, o_ref):
        y = _dot(h_ref[...], w_s[...]) + b_ref[...]
        o_ref[...] = y.astype(o_ref.dtype).reshape(o_ref.shape)

    _stream_expert_blocks(fs_ref[e], ns_ref[e], tot_ref[0], n_blocks, h_hbm, y_hbm, None,
                          hbuf, obuf, hsem, osem, compute)


def _expert_down(first_blk, n_blk, total, h, w_down, b_down):
    n_exp, ff, dm = w_down.shape
    nr = h.shape[0]
    tile = (dm // LANES, LANES)
    return pl.pallas_call(
        functools.partial(_down_body, n_blocks=nr // ROW_BLOCK),
        grid_spec=pltpu.PrefetchScalarGridSpec(
            num_scalar_prefetch=3,
            grid=(1, n_exp),
            in_specs=[pl.BlockSpec(memory_space=pl.ANY),
                      pl.BlockSpec((None, ff, dm), lambda j, e, fs, ns, tot: (e, 0, 0)),
                      pl.BlockSpec((None, 1, dm), lambda j, e, fs, ns, tot: (e, 0, 0))],
            out_specs=pl.BlockSpec(memory_space=pl.ANY),
            scratch_shapes=[pltpu.VMEM((ff, dm), BF16),
                            pltpu.VMEM((IN_SLOTS, ROW_BLOCK, ff), h.dtype), pltpu.VMEM((2, ROW_BLOCK) + tile, BF16),
                            pltpu.SemaphoreType.DMA((IN_SLOTS,)), pltpu.SemaphoreType.DMA((2,))]),
        out_shape=jax.ShapeDtypeStruct((nr,) + tile, BF16),
        compiler_params=_cparams(("arbitrary", "arbitrary"), 56),
        name="moe_down",
    )(first_blk, n_blk, total, h, w_down, b_down)


def _combine_body(pos_ref, x1_ref, gt_ref, ga_ref, gf_ref, y_hbm, o_ref, ybuf, sem, *, tm):
    i = pl.program_id(0)
    n = pl.num_programs(0)

    def gather(tile, slot):
        def issue(r, carry):
            for k in range(TOP_K):
                p = pos_ref[(tile * tm + r) * TOP_K + k]
                _row_copy(y_hbm, p, ybuf.at[slot, k], r, sem.at[slot]).start(priority=k % 2)
            return carry
        lax.fori_loop(0, tm, issue, 0, unroll=8)

    @pl.when(i == 0)
    def _():
        gather(0, 0)

    @pl.when(i + 1 < n)
    def _():
        gather(i + 1, (i + 1) % 2)

    slot = i % 2

    def drain(r, carry):
        for k in range(TOP_K):
            _row_copy(y_hbm, 0, ybuf.at[slot, k], 0, sem.at[slot]).wait()
        return carry

    lax.fori_loop(0, tm, drain, 0, unroll=8)

    gt = gt_ref[0]
    dm = x1_ref.shape[2]
    rows_of = lambda k: ybuf[slot, k].reshape(tm, dm).astype(F32)
    acc = gt[:, 0:1] * rows_of(0)
    for k in range(1, TOP_K):
        acc = acc + gt[:, k:k + 1] * rows_of(k)
    x2 = x1_ref[0] + ga_ref[0] * acc
    ms = jnp.mean(x2 * x2, axis=-1, keepdims=True)
    o_ref[0] = x2 * lax.rsqrt(ms + EPS) * gf_ref[...]


def _combine(pos_flat, x1, gate, ga2, g_final, y, tm=128):
    b, l, dm = x1.shape
    per = l // tm
    tok = lambda width: pl.BlockSpec((1, tm, width), lambda i, pos: (i // per, i % per, 0))
    return pl.pallas_call(
        functools.partial(_combine_body, tm=tm),
        grid_spec=pltpu.PrefetchScalarGridSpec(
            num_scalar_prefetch=1,
            grid=(b * per,),
            in_specs=[tok(dm), tok(LANES),
                      pl.BlockSpec((1, 1, dm), lambda i, pos: (i // per, 0, 0)),
                      pl.BlockSpec((1, dm), lambda i, pos: (0, 0)),
                      pl.BlockSpec(memory_space=pl.ANY)],
            out_specs=tok(dm),
            scratch_shapes=[pltpu.VMEM((2, TOP_K, tm) + y.shape[1:], y.dtype), pltpu.SemaphoreType.DMA((2,))]),
        out_shape=jax.ShapeDtypeStruct((b, l, dm), F32),
        compiler_params=_cparams(("arbitrary",), 48),
        name="moe_combine",
    )(pos_flat, x1, gate, ga2, g_final, y)


def _filter_features(length):
    t = jnp.linspace(0.0, 1.0, length, dtype=F32)[:, None]
    bands = jnp.linspace(1e-4, HY_BANDS - 1, HY_BANDS, dtype=F32)
    ang = (2 * math.pi / length) * jnp.arange(length, dtype=F32)[:, None] * bands
    z = jnp.concatenate([t, jnp.cos(ang), -jnp.sin(ang)], axis=-1)
    return jnp.pad(z, ((0, 0), (0, LANES - z.shape[1])))


def _segments(counts):
    padded = (counts + ROW_BLOCK - 1) // ROW_BLOCK * ROW_BLOCK
    seg_end = jnp.cumsum(padded)
    seg_start = seg_end - padded
    total = (seg_end[-1:] // ROW_BLOCK).astype(I32)
    return (seg_start, (seg_start // ROW_BLOCK).astype(I32), (padded // ROW_BLOCK).astype(I32), total,
            seg_start + counts, padded - counts)


def kernel(x, c, ctx, c_ctx, w_ada, b_ada, g_norm1, g_norm2, w_in, s5_lam_re, s5_lam_im, s5_log_step,
           s5_b_re, s5_b_im, s5_c_re, s5_c_im, s5_d, s5_w_glu, s5_b_glu, hy_conv_w, hy_conv_b,
           hy_w1, hy_b1, hy_w2, hy_b2, hy_w3, hy_b3, hy_w4, hy_freq, hy_deltas, hy_d, w_out,
           w_router, b_router, w_gate_up, b_gate_up, w_down, b_down, g_final):
    assert w_ada.shape[0] == 1, "single-layer block"
    b, l, dm = x.shape
    lc = ctx.shape[1]
    s5w = s5_d.shape[1]
    hyw = hy_d.shape[1]
    n_exp = w_router.shape[2]
    row = lambda a: a.reshape(1, -1)

    cc = jnp.concatenate([c, c_ctx[None], jnp.zeros((2 * SUBLANES - b - 1, dm), F32)], axis=0)
    mod = _adaln(cc, w_ada[0], row(b_ada[0]))
    sh1, sc1, ga1, sh2, sc2, ga2 = [mod[:b, k * dm:(k + 1) * dm].reshape(b, 1, dm) for k in range(N_MOD)]
    csh1, csc1 = [mod[b:b + 1, k * dm:(k + 1) * dm].reshape(1, 1, dm) for k in range(2)]

    w_in_b = w_in[0].astype(BF16)
    g1 = row(g_norm1[0])
    p = _inproj(x, g1, sh1, sc1, w_in_b, s5w, tm=512)
    p_ctx = _inproj(ctx, g1, csh1, csc1, w_in_b, s5w, tm=lc)
    v, x0 = _inproj_hyena(x, g1, sh1, sc1, w_in_b, hy_conv_w[0], row(hy_conv_b[0]), hyw, s5w)

    bw, cw, lamr, lami = _s5_weights(s5_lam_re[0], s5_lam_im[0], s5_log_step[0],
                                     s5_b_re[0], s5_b_im[0], s5_c_re[0], s5_c_im[0])
    zero_state = jnp.zeros((2, bw.shape[1], SUBLANES, 2 * MODES_PER_BLOCK), F32)
    (ctx_state,) = _s5_scan(p_ctx, bw, cw, lamr, lami, zero_state, tl=64, emit_y=False)
    y_s5, _ = _s5_scan(p, bw, cw, lamr, lami, ctx_state, tl=64, emit_y=True)

    ka, kd = _hyena_filter(_filter_features(l), jnp.pad(hy_w1[0], ((0, LANES - hy_w1.shape[1]), (0, 0))),
                           row(hy_b1[0]), hy_w2[0], row(hy_b2[0]), hy_w3[0], row(hy_b3[0]),
                           row(hy_freq[0]), hy_w4[0], row(hy_deltas[0]))
    ctab, stab = _dft_tables(l)
    kk, kn = _hyena_spectrum(ctab, stab, ka, kd)
    y_hy = _hyena_conv(v, x0, kk, kn, row(hy_d[0]), ctab, stab)

    w_rh, w_rl = _split_bf16(jnp.pad(w_router[0], ((0, 0), (0, LANES - n_exp))))
    b_r = jnp.concatenate([b_router[0], jnp.full((LANES - n_exp,), NEG_BIG, F32)]).reshape(1, LANES)
    x1, h2, te, gate, cnt = _outproj(y_s5, p, y_hy, x, row(s5_d[0]), s5_w_glu[0].astype(BF16),
                                     row(s5_b_glu[0]), w_out[0].astype(BF16), ga1, row(g_norm2[0]),
                                     sh2, sc2, w_rh, w_rl, b_r)

    t = b * l
    n_blocks = t * TOP_K // ROW_BLOCK + n_exp
    counts = cnt[0, :n_exp].astype(I32)
    seg_start, first_blk, n_blk, total, pad_start, pad_len = _segments(counts)
    seg_row = jnp.broadcast_to(jnp.pad(seg_start.astype(F32), (0, LANES - n_exp)), (SUBLANES, LANES))
    pos = _slots(te.reshape(t, LANES), seg_row)
    pos_flat = pos[:, :TOP_K].reshape(-1)
    xs = _dispatch(pos_flat, pad_start, pad_len, h2.reshape(t, dm // LANES, LANES), n_blocks * ROW_BLOCK)
    hid = _expert_gateup(first_blk, n_blk, total, xs, w_gate_up[0], b_gate_up[0].reshape(n_exp, 1, -1))
    y_e = _expert_down(first_blk, n_blk, total, hid, w_down[0], b_down[0].reshape(n_exp, 1, -1))
    return _combine(pos_flat, x1, gate, ga2, row(g_final), y_e)
```
